```python
import math
import jax, jax.numpy as jnp
from jax import lax
import numpy as np

D_MODEL = 1024
BATCH = 32
SEQ = 2048
DEPTH = 1

MEM_LEN = 256
D_FF = 2816
CONV_CH = D_MODEL
CONV_WIDTH = 31
DIFF_HEADS = 8
DIFF_HEAD_DIM = 64
DIFF_WIDTH = DIFF_HEADS * 2 * DIFF_HEAD_DIM
MEM_HEADS = 4
MEM_HEAD_DIM = 256
MEM_WIDTH = MEM_HEADS * MEM_HEAD_DIM
N_BRANCH = 3
ROPE_THETA = 10000.0
Q_BLOCK = 128
EPS = 1e-6
IN_COLS = 2 * CONV_CH + 3 * DIFF_WIDTH + MEM_WIDTH + N_BRANCH * D_MODEL

kernel_name = "hybrid_conformer_diffattn_memory_block"


def rmsnorm(x, g):
    xf = x.astype(jnp.float32)
    y = xf * lax.rsqrt(jnp.mean(xf * xf, axis=-1, keepdims=True) + EPS)
    return (y * g.astype(jnp.float32)).astype(x.dtype)


def layernorm(x, g, b):
    xf = x.astype(jnp.float32)
    mu = jnp.mean(xf, axis=-1, keepdims=True)
    var = jnp.mean(jnp.square(xf - mu), axis=-1, keepdims=True)
    y = (xf - mu) * lax.rsqrt(var + EPS)
    return (y * g.astype(jnp.float32) + b.astype(jnp.float32)).astype(x.dtype)


def swiglu(h, w_gu, w_down):
    gate, up = jnp.split(h @ w_gu, 2, axis=-1)
    return (jax.nn.silu(gate) * up) @ w_down


def rope_tables(seq, dim, dtype):
    inv_freq = 1.0 / (ROPE_THETA ** (jnp.arange(0, dim, 2, dtype=jnp.float32) / dim))
    ang = jnp.arange(seq, dtype=jnp.float32)[:, None] * inv_freq[None, :]
    return jnp.cos(ang).astype(dtype), jnp.sin(ang).astype(dtype)


def apply_rope(x, cos, sin):
    half = x.shape[-1] // 2
    x1, x2 = x[..., :half], x[..., half:]
    c, s = cos[:, None, :], sin[:, None, :]
    return jnp.concatenate([x1 * c - x2 * s, x1 * s + x2 * c], axis=-1)


def conv_module(glu_in, conv_w, conv_b, ln_g, ln_b, w_out):
    a, b = jnp.split(glu_in, 2, axis=-1)
    u = a * jax.nn.sigmoid(b)
    pad = CONV_WIDTH // 2
    u = lax.conv_general_dilated(u, conv_w.astype(u.dtype), window_strides=(1,),
                                 padding=[(pad, pad)],
                                 dimension_numbers=('NWC', 'WIO', 'NWC'),
                                 feature_group_count=CONV_CH) + conv_b
    u = jax.nn.silu(layernorm(u, ln_g, ln_b))
    return u @ w_out


def diff_attention(q, k, v, lam):
    B, S = q.shape[0], q.shape[1]
    nb = S // Q_BLOCK
    qb = q.reshape(B, nb, Q_BLOCK, DIFF_HEADS, 2, DIFF_HEAD_DIM).transpose(1, 0, 2, 3, 4, 5)
    scale = DIFF_HEAD_DIM ** -0.5

    def block(qblk):
        s = jnp.einsum('bqhcd,bkhcd->bhcqk', qblk, k,
                       preferred_element_type=jnp.float32) * scale
        p = jax.nn.softmax(s, axis=-1)
        a = p[:, :, 0] - lam * p[:, :, 1]
        return jnp.einsum('bhqk,bkhe->bqhe', a.astype(v.dtype), v)

    o = lax.map(block, qb)
    return o.transpose(1, 0, 2, 3, 4).reshape(B, S, DIFF_HEADS, 2 * DIFF_HEAD_DIM)


def memory_attention(q, k, v):
    s = jnp.einsum('bqhd,bkhd->bhqk', q, k,
                   preferred_element_type=jnp.float32) * (MEM_HEAD_DIM ** -0.5)
    p = jax.nn.softmax(s, axis=-1)
    return jnp.einsum('bhqk,bkhd->bqhd', p.astype(v.dtype), v)


def setup_inputs(seed: int = 0) -> dict:
    key = jax.random.key(seed)
    ks = jax.random.split(key, 32)
    L, D = DEPTH, D_MODEL
    f32 = jnp.float32

    def w(k, shape, fan_in):
        return jax.random.normal(k, shape, f32) * (fan_in ** -0.5)

    def gain(k, shape):
        return 1.0 + 0.02 * jax.random.normal(k, shape, f32)

    def small(k, shape, s=0.02):
        return s * jax.random.normal(k, shape, f32)

    return {
        'x': jax.random.normal(ks[0], (BATCH, SEQ, D), f32),
        'mem': jax.random.normal(ks[1], (BATCH, MEM_LEN, D), f32),
        'ffn1_norm': gain(ks[2], (L, D)),
        'ffn1_w_gu': w(ks[3], (L, D, 2 * D_FF), D),
        'ffn1_w_down': w(ks[4], (L, D_FF, D), D_FF),
        'mix_norm': gain(ks[5], (L, D)),
        'mem_norm': gain(ks[6], (L, D)),
        'w_in': w(ks[7], (L, D, IN_COLS), D),
        'b_gate': small(ks[8], (L, N_BRANCH * D)),
        'conv_w': w(ks[9], (L, CONV_WIDTH, 1, CONV_CH), CONV_WIDTH),
        'conv_b': small(ks[10], (L, CONV_CH)),
        'conv_ln_g': gain(ks[11], (L, CONV_CH)),
        'conv_ln_b': small(ks[12], (L, CONV_CH)),
        'w_conv_out': w(ks[13], (L, CONV_CH, D), CONV_CH),
        'diff_q_norm': gain(ks[14], (L, DIFF_HEAD_DIM)),
        'diff_k_norm': gain(ks[15], (L, DIFF_HEAD_DIM)),
        'diff_lambda': small(ks[16], (L, 4, DIFF_HEAD_DIM), 0.1),
        'diff_subln': gain(ks[17], (L, 2 * DIFF_HEAD_DIM)),
        'w_diff_out': w(ks[18], (L, DIFF_WIDTH, D), DIFF_WIDTH),
        'w_mem_kv': w(ks[19], (L, D, 2 * MEM_WIDTH), D),
        'mem_q_norm': gain(ks[20], (L, MEM_HEAD_DIM)),
        'mem_k_norm': gain(ks[21], (L, MEM_HEAD_DIM)),
        'w_mem_out': w(ks[22], (L, MEM_WIDTH, D), MEM_WIDTH),
        'w_o': w(ks[23], (L, D, D), D),
        'ffn2_norm': gain(ks[24], (L, D)),
        'ffn2_w_gu': w(ks[25], (L, D, 2 * D_FF), D),
        'ffn2_w_down': w(ks[26], (L, D_FF, D), D_FF),
    }


def reference(x, mem, ffn1_norm, ffn1_w_gu, ffn1_w_down, mix_norm, mem_norm, w_in, b_gate,
              conv_w, conv_b, conv_ln_g, conv_ln_b, w_conv_out, diff_q_norm, diff_k_norm,
              diff_lambda, diff_subln, w_diff_out, w_mem_kv, mem_q_norm, mem_k_norm,
              w_mem_out, w_o, ffn2_norm, ffn2_w_gu, ffn2_w_down):
    B, S, D = x.shape
    M = mem.shape[1]
    cos, sin = rope_tables(S, DIFF_HEAD_DIM, x.dtype)
    c1 = 2 * CONV_CH
    c2 = c1 + DIFF_WIDTH
    c3 = c2 + DIFF_WIDTH
    c4 = c3 + DIFF_WIDTH
    c5 = c4 + MEM_WIDTH
    for l in range(DEPTH):
        lam_init = 0.8 - 0.6 * math.exp(-0.3 * l)
        x = x + 0.5 * swiglu(rmsnorm(x, ffn1_norm[l]), ffn1_w_gu[l], ffn1_w_down[l])

        h = rmsnorm(x, mix_norm[l])
        proj = h @ w_in[l]
        glu_in, dq, dk, dv, mq, gl = jnp.split(proj, [c1, c2, c3, c4, c5], axis=-1)

        y_conv = conv_module(glu_in, conv_w[l], conv_b[l], conv_ln_g[l], conv_ln_b[l],
                             w_conv_out[l])

        q = apply_rope(rmsnorm(dq.reshape(B, S, 2 * DIFF_HEADS, DIFF_HEAD_DIM), diff_q_norm[l]), cos, sin)
        k = apply_rope(rmsnorm(dk.reshape(B, S, 2 * DIFF_HEADS, DIFF_HEAD_DIM), diff_k_norm[l]), cos, sin)
        q = q.reshape(B, S, DIFF_HEADS, 2, DIFF_HEAD_DIM)
        k = k.reshape(B, S, DIFF_HEADS, 2, DIFF_HEAD_DIM)
        v = dv.reshape(B, S, DIFF_HEADS, 2 * DIFF_HEAD_DIM)
        lp = diff_lambda[l].astype(jnp.float32)
        lam = jnp.exp(jnp.sum(lp[0] * lp[1])) - jnp.exp(jnp.sum(lp[2] * lp[3])) + lam_init
        o = diff_attention(q, k, v, lam)
        o = rmsnorm(o, diff_subln[l]) * (1.0 - lam_init)
        y_diff = o.reshape(B, S, DIFF_WIDTH) @ w_diff_out[l]

        kv = rmsnorm(mem, mem_norm[l]) @ w_mem_kv[l]
        mk, mv = jnp.split(kv, 2, axis=-1)
        mk = rmsnorm(mk.reshape(B, M, MEM_HEADS, MEM_HEAD_DIM), mem_k_norm[l])
        mv = mv.reshape(B, M, MEM_HEADS, MEM_HEAD_DIM)
        mqh = rmsnorm(mq.reshape(B, S, MEM_HEADS, MEM_HEAD_DIM), mem_q_norm[l])
        y_mem = memory_attention(mqh, mk, mv).reshape(B, S, MEM_WIDTH) @ w_mem_out[l]

        g = jax.nn.sigmoid((gl + b_gate[l]).astype(jnp.float32)).astype(x.dtype)
        g = g.reshape(B, S, N_BRANCH, D)
        merged = g[:, :, 0] * y_conv + g[:, :, 1] * y_diff + g[:, :, 2] * y_mem
        x = x + merged @ w_o[l]

        x = x + 0.5 * swiglu(rmsnorm(x, ffn2_norm[l]), ffn2_w_gu[l], ffn2_w_down[l])
    return x
```

```python
import functools
import math

import jax
import jax.numpy as jnp
from jax import lax
from jax.experimental import pallas as pl
from jax.experimental.pallas import tpu as pltpu

F32 = jnp.float32
BF16 = jnp.bfloat16

D_MODEL = 1024
D_FF = 2816
CONV_WIDTH = 31
CONV_PAD = CONV_WIDTH // 2
DIFF_HEADS = 8
DIFF_HEAD_DIM = 64
MEM_HEADS = 4
MEM_HEAD_DIM = 256
ROPE_THETA = 10000.0
EPS = 1e-6
LAM_INIT = 0.8 - 0.6 * math.exp(-0.3 * 0)

LANES = 128
BF16_SUBLANES = 16
FF_CHUNK = 256
N_FF_CHUNKS = D_FF // FF_CHUNK
VMEM_LIMIT = 56 * 1024 * 1024

TM = 512
TQ = 256
TQ_MEM = 512
TS = 256


def _params(n_axes):
    return pltpu.CompilerParams(
        dimension_semantics=("arbitrary",) * n_axes,
        vmem_limit_bytes=VMEM_LIMIT)


def _resident(shape):
    zeros = (0,) * len(shape)
    return pl.BlockSpec(shape, lambda *_: zeros, pipeline_mode=pl.Buffered(1))


def _rms(x, gain):
    return x * lax.rsqrt(jnp.mean(x * x, axis=-1, keepdims=True) + EPS) * gain


def _mm(a, b):
    return jnp.dot(a, b, preferred_element_type=F32)


def _mm_nt(a, b):
    return lax.dot_general(a, b, (((1,), (1,)), ((), ())),
                           preferred_element_type=F32)


def _ffn_body(emit_h, x_ref, g_ref, wg_ref, wu_ref, wd_ref, *rest):
    if emit_h:
        g2_ref, o_ref, h_ref, a_ref = rest
    else:
        o_ref, a_ref = rest
    x = x_ref[...]
    xn = _rms(x, g_ref[...]).astype(BF16)
    for c in range(N_FF_CHUNKS):
        gate = _mm(xn, wg_ref[c])
        up = _mm(xn, wu_ref[c])
        a_ref[:, c * FF_CHUNK:(c + 1) * FF_CHUNK] = (
            gate * jax.nn.sigmoid(gate) * up).astype(BF16)
    x1 = x + 0.5 * _mm(a_ref[...], wd_ref[...])
    o_ref[...] = x1
    if emit_h:
        h_ref[...] = _rms(x1, g2_ref[...]).astype(BF16)


def _ffn(x, gain, w_gu, w_down, next_gain=None):
    n = x.shape[0]
    emit_h = next_gain is not None
    wg = w_gu[:, :D_FF].reshape(D_MODEL, N_FF_CHUNKS, FF_CHUNK)
    wu = w_gu[:, D_FF:].reshape(D_MODEL, N_FF_CHUNKS, FF_CHUNK)
    wg = wg.transpose(1, 0, 2).astype(BF16)
    wu = wu.transpose(1, 0, 2).astype(BF16)
    wd = w_down.astype(BF16)
    row = pl.BlockSpec((TM, D_MODEL), lambda i: (i, 0))
    in_specs = [row, _resident((1, D_MODEL)), _resident(wg.shape),
                _resident(wu.shape), _resident(wd.shape)]
    args = [x, gain.reshape(1, D_MODEL), wg, wu, wd]
    out_shape = [jax.ShapeDtypeStruct((n, D_MODEL), F32)]
    out_specs = [row]
    if emit_h:
        in_specs.append(_resident((1, D_MODEL)))
        args.append(next_gain.reshape(1, D_MODEL))
        out_shape.append(jax.ShapeDtypeStruct((n, D_MODEL), BF16))
        out_specs.append(row)
    out = pl.pallas_call(
        functools.partial(_ffn_body, emit_h),
        grid=(n // TM,),
        in_specs=in_specs, out_specs=out_specs, out_shape=out_shape,
        scratch_shapes=[pltpu.VMEM((TM, D_FF), BF16)],
        compiler_params=_params(1), name="ffn")(*args)
    return out if emit_h else out[0]


def _glu_body(h_ref, wa_ref, wb_ref, u_ref):
    h = h_ref[...]
    a = _mm(h, wa_ref[...])
    b = _mm(h, wb_ref[...])
    u_ref[...] = (a * jax.nn.sigmoid(b)).astype(BF16)


def _glu(h, wa, wb):
    n = h.shape[0]
    row = pl.BlockSpec((TM, D_MODEL), lambda i: (i, 0))
    return pl.pallas_call(
        _glu_body, grid=(n // TM,),
        in_specs=[row, _resident(wa.shape), _resident(wb.shape)],
        out_specs=row,
        out_shape=jax.ShapeDtypeStruct((n, D_MODEL), BF16),
        compiler_params=_params(1), name="glu")(h, wa, wb)


def _head_norm_rope(y, gain, group_mean, cos, sin_signed, scale):
    lane = lax.broadcasted_iota(jnp.int32, (y.shape[0], LANES), 1)
    upper_half = (lane & (DIFF_HEAD_DIM // 2)) != 0
    outs = []
    width = group_mean.shape[0]
    for g in range(y.shape[1] // width):
        yg = y[:, g * width:(g + 1) * width]
        ms = _mm((yg * yg).astype(BF16), group_mean)
        yn = yg * lax.rsqrt(ms + EPS) * gain[:, g * width:(g + 1) * width]
        for j in range(width // LANES):
            t = yn[:, j * LANES:(j + 1) * LANES]
            swapped = jnp.where(upper_half,
                                pltpu.roll(t, DIFF_HEAD_DIM // 2, 1),
                                pltpu.roll(t, LANES - DIFF_HEAD_DIM // 2, 1))
            outs.append(((t * cos + swapped * sin_signed) * scale).astype(BF16))
    return jnp.concatenate(outs, axis=1)


def _qk_body(h_ref, wq_ref, wk_ref, gq_ref, gk_ref, gm_ref, cos_ref, sin_ref,
             q_ref, k_ref):
    h = h_ref[...]
    cos = cos_ref[...]
    sin_signed = sin_ref[...]
    gm = gm_ref[...]
    q_ref[...] = _head_norm_rope(_mm(h, wq_ref[...]), gq_ref[...], gm, cos,
                                 sin_signed, DIFF_HEAD_DIM ** -0.5)
    k_ref[...] = _head_norm_rope(_mm(h, wk_ref[...]), gk_ref[...], gm, cos,
                                 sin_signed, 1.0)


def _qk(h, wq, wk, gq, gk, seq):
    n = h.shape[0]
    half = DIFF_HEAD_DIM // 2
    inv_freq = 1.0 / (ROPE_THETA ** (jnp.arange(0, DIFF_HEAD_DIM, 2, dtype=F32)
                                     / DIFF_HEAD_DIM))
    ang = jnp.arange(seq, dtype=F32)[:, None] * inv_freq[None, :]
    cos, sin = jnp.cos(ang), jnp.sin(ang)
    cos_t = jnp.tile(cos, (1, LANES // half))
    sin_t = jnp.tile(jnp.concatenate([-sin, sin], axis=1), (1, LANES // DIFF_HEAD_DIM))
    width = 2 * LANES
    ids = jnp.arange(width) // DIFF_HEAD_DIM
    group_mean = jnp.where(ids[:, None] == ids[None, :], 1.0 / DIFF_HEAD_DIM,
                           0.0).astype(BF16)
    reps = D_MODEL // DIFF_HEAD_DIM
    row = pl.BlockSpec((TM, D_MODEL), lambda i: (i, 0))
    tab = pl.BlockSpec((TM, LANES), lambda i: (i % (seq // TM), 0))
    out = jax.ShapeDtypeStruct((n, D_MODEL), BF16)
    return pl.pallas_call(
        _qk_body, grid=(n // TM,),
        in_specs=[row, _resident(wq.shape), _resident(wk.shape),
                  _resident((1, D_MODEL)), _resident((1, D_MODEL)),
                  _resident(group_mean.shape), tab, tab],
        out_specs=[row, row], out_shape=[out, out],
        compiler_params=_params(1), name="qk")(
            h, wq, wk, jnp.tile(gq, reps).reshape(1, D_MODEL),
            jnp.tile(gk, reps).reshape(1, D_MODEL), group_mean, cos_t, sin_t)


def _vmq_body(h_ref, wv_ref, wm_ref, gm_ref, v_ref, mq_ref):
    h = h_ref[...]
    v_ref[...] = _mm(h, wv_ref[...]).astype(BF16)
    y = _mm(h, wm_ref[...])
    scale = MEM_HEAD_DIM ** -0.5
    for g in range(MEM_HEADS):
        sl = slice(g * MEM_HEAD_DIM, (g + 1) * MEM_HEAD_DIM)
        mq_ref[:, sl] = (_rms(y[:, sl], gm_ref[...]) * scale).astype(BF16)


def _vmq(h, wv, wm, gm):
    n = h.shape[0]
    row = pl.BlockSpec((TM, D_MODEL), lambda i: (i, 0))
    out = jax.ShapeDtypeStruct((n, D_MODEL), BF16)
    return pl.pallas_call(
        _vmq_body, grid=(n // TM,),
        in_specs=[row, _resident(wv.shape), _resident(wm.shape),
                  _resident((1, MEM_HEAD_DIM))],
        out_specs=[row, row], out_shape=[out, out],
        compiler_params=_params(1), name="vmq")(
            h, wv, wm, gm.reshape(1, MEM_HEAD_DIM))


def _conv_body(prev_ref, main_ref, next_ref, w_ref, cb_ref, lg_ref, lb_ref,
               o_ref, xp_ref, y_ref):
    i = pl.program_id(1)
    halo = BF16_SUBLANES
    prev = prev_ref[0].astype(F32)
    nxt = next_ref[0].astype(F32)
    xp_ref[0:halo, :] = jnp.where(i > 0, prev, 0.0)
    xp_ref[halo:halo + TS, :] = main_ref[0].astype(F32)
    xp_ref[halo + TS:, :] = jnp.where(i < pl.num_programs(1) - 1, nxt, 0.0)
    base = halo - CONV_PAD
    for j in range(D_MODEL // LANES):
        lanes = slice(j * LANES, (j + 1) * LANES)
        acc = jnp.zeros((TS, LANES), F32)
        for k in range(CONV_WIDTH):
            acc = acc + xp_ref[base + k:base + k + TS, lanes] * w_ref[k:k + 1, lanes]
        y_ref[:, lanes] = acc
    y = y_ref[...] + cb_ref[...]
    mu = jnp.mean(y, axis=-1, keepdims=True)
    yc = y - mu
    var = jnp.mean(yc * yc, axis=-1, keepdims=True)
    z = yc * lax.rsqrt(var + EPS) * lg_ref[...] + lb_ref[...]
    o_ref[0] = (z * jax.nn.sigmoid(z)).astype(BF16)


def _conv(u, conv_w, conv_b, ln_g, ln_b):
    b, s, c = u.shape
    halo = BF16_SUBLANES
    per = TS // halo
    last = s // halo - 1
    main = pl.BlockSpec((1, TS, c), lambda bi, i: (bi, i, 0))
    prev = pl.BlockSpec((1, halo, c), lambda bi, i: (bi, jnp.maximum(i * per - 1, 0), 0))
    nxt = pl.BlockSpec((1, halo, c), lambda bi, i: (bi, jnp.minimum((i + 1) * per, last), 0))
    return pl.pallas_call(
        _conv_body, grid=(b, s // TS),
        in_specs=[prev, main, nxt, _resident((CONV_WIDTH, c)), _resident((1, c)),
                  _resident((1, c)), _resident((1, c))],
        out_specs=main, out_shape=jax.ShapeDtypeStruct((b, s, c), BF16),
        scratch_shapes=[pltpu.VMEM((TS + 2 * halo, c), F32),
                        pltpu.VMEM((TS, c), F32)],
        compiler_params=_params(2), name="conv")(
            u, u, u, conv_w.reshape(CONV_WIDTH, c), conv_b.reshape(1, c),
            ln_g.reshape(1, c), ln_b.reshape(1, c))


def _diff_body(lam_ref, sg_ref, q_ref, k_ref, v_ref, o_ref):
    lp = lam_ref[...]
    lam = (jnp.exp(jnp.sum(lp[0:1] * lp[1:2], axis=-1, keepdims=True))
           - jnp.exp(jnp.sum(lp[2:3] * lp[3:4], axis=-1, keepdims=True)) + LAM_INIT)
    q = q_ref[0]
    lane = lax.broadcasted_iota(jnp.int32, q.shape, 1)
    zero = jnp.zeros_like(q)
    qq = jnp.concatenate([jnp.where(lane < DIFF_HEAD_DIM, q, zero),
                          jnp.where(lane < DIFF_HEAD_DIM, zero, q)], axis=0)
    s = _mm_nt(qq, k_ref[0])
    e = jnp.exp(s - jnp.max(s, axis=-1, keepdims=True))
    l = jnp.sum(e, axis=-1, keepdims=True)
    o = _mm(e.astype(BF16), v_ref[0]) / l
    od = o[:TQ] - lam * o[TQ:]
    o_ref[0] = (_rms(od, sg_ref[...]) * (1.0 - LAM_INIT)).astype(BF16)


def _diff_attention(q, k, v, diff_lambda, subln):
    b, s, _ = q.shape
    width = 2 * DIFF_HEAD_DIM
    qblk = pl.BlockSpec((1, TQ, width), lambda bi, hi, i: (bi, i, hi))
    kvblk = pl.BlockSpec((1, s, width), lambda bi, hi, i: (bi, 0, hi))
    return pl.pallas_call(
        _diff_body, grid=(b, DIFF_HEADS, s // TQ),
        in_specs=[_resident((4, DIFF_HEAD_DIM)), _resident((1, width)),
                  qblk, kvblk, kvblk],
        out_specs=qblk, out_shape=jax.ShapeDtypeStruct(q.shape, BF16),
        compiler_params=_params(3), name="diffattn")(
            diff_lambda, subln.reshape(1, width), q, k, v)


def _memkv_body(m_ref, g_ref, wk_ref, wv_ref, gk_ref, k_ref, v_ref):
    xn = _rms(m_ref[...], g_ref[...]).astype(BF16)
    v_ref[...] = _mm(xn, wv_ref[...]).astype(BF16)
    y = _mm(xn, wk_ref[...])
    for g in range(MEM_HEADS):
        sl = slice(g * MEM_HEAD_DIM, (g + 1) * MEM_HEAD_DIM)
        k_ref[:, sl] = _rms(y[:, sl], gk_ref[...]).astype(BF16)


def _memkv(mem, gain, wk, wv, gk):
    n = mem.shape[0]
    row = pl.BlockSpec((TM, D_MODEL), lambda i: (i, 0))
    out = jax.ShapeDtypeStruct((n, D_MODEL), BF16)
    return pl.pallas_call(
        _memkv_body, grid=(n // TM,),
        in_specs=[row, _resident((1, D_MODEL)), _resident(wk.shape),
                  _resident(wv.shape), _resident((1, MEM_HEAD_DIM))],
        out_specs=[row, row], out_shape=[out, out],
        compiler_params=_params(1), name="memkv")(
            mem, gain.reshape(1, D_MODEL), wk, wv, gk.reshape(1, MEM_HEAD_DIM))


def _memattn_body(q_ref, k_ref, v_ref, o_ref):
    for g in range(MEM_HEADS):
        sl = slice(g * MEM_HEAD_DIM, (g + 1) * MEM_HEAD_DIM)
        s = _mm_nt(q_ref[0, :, sl], k_ref[0, :, sl])
        e = jnp.exp(s - jnp.max(s, axis=-1, keepdims=True))
        l = jnp.sum(e, axis=-1, keepdims=True)
        o_ref[0, :, sl] = (_mm(e.astype(BF16), v_ref[0, :, sl]) / l).astype(BF16)


def _memattn(q, k, v):
    b, s, w = q.shape
    m = k.shape[1]
    qblk = pl.BlockSpec((1, TQ_MEM, w), lambda bi, i: (bi, i, 0))
    kvblk = pl.BlockSpec((1, m, w), lambda bi, i: (bi, 0, 0))
    return pl.pallas_call(
        _memattn_body, grid=(b, s // TQ_MEM),
        in_specs=[qblk, kvblk, kvblk], out_specs=qblk,
        out_shape=jax.ShapeDtypeStruct(q.shape, BF16),
        compiler_params=_params(2), name="memattn")(q, k, v)


def _merge_body(x_ref, h_ref, c_ref, d_ref, m_ref, wc_ref, wd_ref, wm_ref,
                wg_ref, bg_ref, wo_ref, o_ref):
    gates = jax.nn.sigmoid(_mm(h_ref[...], wg_ref[...]) + bg_ref[...])
    merged = (gates[:, :D_MODEL] * _mm(c_ref[...], wc_ref[...])
              + gates[:, D_MODEL:2 * D_MODEL] * _mm(d_ref[...], wd_ref[...])
              + gates[:, 2 * D_MODEL:] * _mm(m_ref[...], wm_ref[...]))
    o_ref[...] = x_ref[...] + _mm(merged.astype(BF16), wo_ref[...])


def _merge(x1, h, c, d, m, wc, wd, wm, wg, bg, wo):
    n = x1.shape[0]
    row = pl.BlockSpec((TM, D_MODEL), lambda i: (i, 0))
    return pl.pallas_call(
        _merge_body, grid=(n // TM,),
        in_specs=[row, row, row, row, row, _resident(wc.shape), _resident(wd.shape),
                  _resident(wm.shape), _resident(wg.shape),
                  _resident((1, 3 * D_MODEL)), _resident(wo.shape)],
        out_specs=row, out_shape=jax.ShapeDtypeStruct((n, D_MODEL), F32),
        compiler_params=_params(1), name="merge")(
            x1, h, c, d, m, wc, wd, wm, wg, bg.reshape(1, 3 * D_MODEL), wo)


def kernel(x, mem, ffn1_norm, ffn1_w_gu, ffn1_w_down, mix_norm, mem_norm, w_in, b_gate,
           conv_w, conv_b, conv_ln_g, conv_ln_b, w_conv_out, diff_q_norm, diff_k_norm,
           diff_lambda, diff_subln, w_diff_out, w_mem_kv, mem_q_norm, mem_k_norm,
           w_mem_out, w_o, ffn2_norm, ffn2_w_gu, ffn2_w_down):
    b, s, d = x.shape
    m = mem.shape[1]
    n = b * s
    assert x.shape[2] == D_MODEL and ffn1_norm.shape[0] == 1
    assert n % TM == 0 and s % TM == 0 and s % TQ == 0 and s % TS == 0
    assert (b * m) % TM == 0 and s % TQ_MEM == 0

    w_in_b = w_in[0].astype(BF16)
    c1 = 2 * D_MODEL
    cols = [w_in_b[:, i * D_MODEL:(i + 1) * D_MODEL] for i in range(6)]
    w_gate = w_in_b[:, 6 * D_MODEL:]

    x1, h = _ffn(x.reshape(n, d), ffn1_norm[0], ffn1_w_gu[0], ffn1_w_down[0],
                 next_gain=mix_norm[0])

    u = _glu(h, cols[0], cols[1])
    cact = _conv(u.reshape(b, s, d), conv_w[0], conv_b[0], conv_ln_g[0], conv_ln_b[0])

    q, k = _qk(h, cols[2], cols[3], diff_q_norm[0], diff_k_norm[0], s)
    v, mq = _vmq(h, cols[4], cols[5], mem_q_norm[0])
    od = _diff_attention(q.reshape(b, s, d), k.reshape(b, s, d), v.reshape(b, s, d),
                         diff_lambda[0], diff_subln[0])

    w_kv = w_mem_kv[0].astype(BF16)
    mk, mv = _memkv(mem.reshape(b * m, d), mem_norm[0], w_kv[:, :D_MODEL],
                    w_kv[:, D_MODEL:], mem_k_norm[0])
    om = _memattn(mq.reshape(b, s, d), mk.reshape(b, m, d), mv.reshape(b, m, d))

    x2 = _merge(x1, h, cact.reshape(n, d), od.reshape(n, d), om.reshape(n, d),
                w_conv_out[0].astype(BF16), w_diff_out[0].astype(BF16),
                w_mem_out[0].astype(BF16), w_gate, b_gate[0], w_o[0].astype(BF16))

    x3 = _ffn(x2, ffn2_norm[0], ffn2_w_gu[0], ffn2_w_down[0])
    return x3.reshape(b, s, d)
```

```python
import functools
import math

import jax
import jax.numpy as jnp
from jax import lax
from jax.experimental import pallas as pl
from jax.experimental.pallas import tpu as pltpu

F32 = jnp.float32
BF16 = jnp.bfloat16

D_MODEL = 1024
D_FF = 2816
CONV_WIDTH = 31
CONV_PAD = CONV_WIDTH // 2
DIFF_HEADS = 8
DIFF_HEAD_DIM = 64
MEM_HEADS = 4
MEM_HEAD_DIM = 256
ROPE_THETA = 10000.0
EPS = 1e-6
LAM_INIT = 0.8 - 0.6 * math.exp(-0.3 * 0)

LANES = 128
BF16_SUBLANES = 16
FF_CHUNK = 256
N_FF_CHUNKS = D_FF // FF_CHUNK
VMEM_LIMIT = 56 * 1024 * 1024

TM = 512
TQ = 256
TQ_MEM = 512
TS = 256
KEY_CHUNK = 256
MAX_SHIFT_BOUND = 40.0


def _params(n_axes):
    return pltpu.CompilerParams(
        dimension_semantics=("arbitrary",) * n_axes,
        vmem_limit_bytes=VMEM_LIMIT)


def _resident(shape):
    zeros = (0,) * len(shape)
    return pl.BlockSpec(shape, lambda *_: zeros, pipeline_mode=pl.Buffered(1))


def _rms(x, gain):
    return x * lax.rsqrt(jnp.mean(x * x, axis=-1, keepdims=True) + EPS) * gain


def _mm(a, b):
    return jnp.dot(a, b, preferred_element_type=F32)


def _mm_nt(a, b):
    return lax.dot_general(a, b, (((1,), (1,)), ((), ())),
                           preferred_element_type=F32)


def _ffn_body(emit_h, x_ref, g_ref, wg_ref, wu_ref, wd_ref, *rest):
    if emit_h:
        g2_ref, o_ref, h_ref, a_ref = rest
    else:
        o_ref, a_ref = rest
    x = x_ref[...]
    xn = _rms(x, g_ref[...]).astype(BF16)
    for c in range(N_FF_CHUNKS):
        gate = _mm(xn, wg_ref[c])
        up = _mm(xn, wu_ref[c])
        a_ref[:, c * FF_CHUNK:(c + 1) * FF_CHUNK] = (
            gate * jax.nn.sigmoid(gate) * up).astype(BF16)
    x1 = x + 0.5 * _mm(a_ref[...], wd_ref[...])
    o_ref[...] = x1
    if emit_h:
        h_ref[...] = _rms(x1, g2_ref[...]).astype(BF16)


def _ffn(x, gain, w_gu, w_down, next_gain=None):
    n = x.shape[0]
    emit_h = next_gain is not None
    wg = w_gu[:, :D_FF].reshape(D_MODEL, N_FF_CHUNKS, FF_CHUNK)
    wu = w_gu[:, D_FF:].reshape(D_MODEL, N_FF_CHUNKS, FF_CHUNK)
    wg = wg.transpose(1, 0, 2).astype(BF16)
    wu = wu.transpose(1, 0, 2).astype(BF16)
    wd = w_down.astype(BF16)
    row = pl.BlockSpec((TM, D_MODEL), lambda i: (i, 0))
    in_specs = [row, _resident((1, D_MODEL)), _resident(wg.shape),
                _resident(wu.shape), _resident(wd.shape)]
    args = [x, gain.reshape(1, D_MODEL), wg, wu, wd]
    out_shape = [jax.ShapeDtypeStruct((n, D_MODEL), F32)]
    out_specs = [row]
    if emit_h:
        in_specs.append(_resident((1, D_MODEL)))
        args.append(next_gain.reshape(1, D_MODEL))
        out_shape.append(jax.ShapeDtypeStruct((n, D_MODEL), BF16))
        out_specs.append(row)
    out = pl.pallas_call(
        functools.partial(_ffn_body, emit_h),
        grid=(n // TM,),
        in_specs=in_specs, out_specs=out_specs, out_shape=out_shape,
        scratch_shapes=[pltpu.VMEM((TM, D_FF), BF16)],
        compiler_params=_params(1), name="ffn")(*args)
    return out if emit_h else out[0]


def _glu_body(h_ref, wa_ref, wb_ref, u_ref):
    h = h_ref[...]
    a = _mm(h, wa_ref[...])
    b = _mm(h, wb_ref[...])
    u_ref[...] = (a * jax.nn.sigmoid(b)).astype(BF16)


def _glu(h, wa, wb):
    n = h.shape[0]
    row = pl.BlockSpec((TM, D_MODEL), lambda i: (i, 0))
    return pl.pallas_call(
        _glu_body, grid=(n // TM,),
        in_specs=[row, _resident(wa.shape), _resident(wb.shape)],
        out_specs=row,
        out_shape=jax.ShapeDtypeStruct((n, D_MODEL), BF16),
        compiler_params=_params(1), name="glu")(h, wa, wb)


def _head_norm_rope(y, gain, group_mean, cos, sin_signed, scale):
    lane = lax.broadcasted_iota(jnp.int32, (y.shape[0], LANES), 1)
    upper_half = (lane & (DIFF_HEAD_DIM // 2)) != 0
    outs = []
    width = group_mean.shape[0]
    for g in range(y.shape[1] // width):
        yg = y[:, g * width:(g + 1) * width]
        ms = _mm((yg * yg).astype(BF16), group_mean)
        yn = yg * lax.rsqrt(ms + EPS) * gain[:, g * width:(g + 1) * width]
        for j in range(width // LANES):
            t = yn[:, j * LANES:(j + 1) * LANES]
            swapped = jnp.where(upper_half,
                                pltpu.roll(t, DIFF_HEAD_DIM // 2, 1),
                                pltpu.roll(t, LANES - DIFF_HEAD_DIM // 2, 1))
            outs.append(((t * cos + swapped * sin_signed) * scale).astype(BF16))
    return jnp.concatenate(outs, axis=1)


def _qk_body(h_ref, wq_ref, wk_ref, gq_ref, gk_ref, gm_ref, cos_ref, sin_ref,
             q_ref, k_ref):
    h = h_ref[...]
    cos = cos_ref[...]
    sin_signed = sin_ref[...]
    gm = gm_ref[...]
    q_ref[...] = _head_norm_rope(_mm(h, wq_ref[...]), gq_ref[...], gm, cos,
                                 sin_signed, DIFF_HEAD_DIM ** -0.5)
    k_ref[...] = _head_norm_rope(_mm(h, wk_ref[...]), gk_ref[...], gm, cos,
                                 sin_signed, 1.0)


def _qk(h, wq, wk, gq, gk, seq):
    n = h.shape[0]
    half = DIFF_HEAD_DIM // 2
    inv_freq = 1.0 / (ROPE_THETA ** (jnp.arange(0, DIFF_HEAD_DIM, 2, dtype=F32)
                                     / DIFF_HEAD_DIM))
    ang = jnp.arange(seq, dtype=F32)[:, None] * inv_freq[None, :]
    cos, sin = jnp.cos(ang), jnp.sin(ang)
    cos_t = jnp.tile(cos, (1, LANES // half))
    sin_t = jnp.tile(jnp.concatenate([-sin, sin], axis=1), (1, LANES // DIFF_HEAD_DIM))
    width = 2 * LANES
    ids = jnp.arange(width) // DIFF_HEAD_DIM
    group_mean = jnp.where(ids[:, None] == ids[None, :], 1.0 / DIFF_HEAD_DIM,
                           0.0).astype(BF16)
    reps = D_MODEL // DIFF_HEAD_DIM
    row = pl.BlockSpec((TM, D_MODEL), lambda i: (i, 0))
    tab = pl.BlockSpec((TM, LANES), lambda i: (i % (seq // TM), 0))
    out = jax.ShapeDtypeStruct((n, D_MODEL), BF16)
    return pl.pallas_call(
        _qk_body, grid=(n // TM,),
        in_specs=[row, _resident(wq.shape), _resident(wk.shape),
                  _resident((1, D_MODEL)), _resident((1, D_MODEL)),
                  _resident(group_mean.shape), tab, tab],
        out_specs=[row, row], out_shape=[out, out],
        compiler_params=_params(1), name="qk")(
            h, wq, wk, jnp.tile(gq, reps).reshape(1, D_MODEL),
            jnp.tile(gk, reps).reshape(1, D_MODEL), group_mean, cos_t, sin_t)


def _vmq_body(h_ref, wv_ref, wm_ref, gm_ref, v_ref, mq_ref):
    h = h_ref[...]
    v_ref[...] = _mm(h, wv_ref[...]).astype(BF16)
    y = _mm(h, wm_ref[...])
    scale = MEM_HEAD_DIM ** -0.5
    for g in range(MEM_HEADS):
        sl = slice(g * MEM_HEAD_DIM, (g + 1) * MEM_HEAD_DIM)
        mq_ref[:, sl] = (_rms(y[:, sl], gm_ref[...]) * scale).astype(BF16)


def _vmq(h, wv, wm, gm):
    n = h.shape[0]
    row = pl.BlockSpec((TM, D_MODEL), lambda i: (i, 0))
    out = jax.ShapeDtypeStruct((n, D_MODEL), BF16)
    return pl.pallas_call(
        _vmq_body, grid=(n // TM,),
        in_specs=[row, _resident(wv.shape), _resident(wm.shape),
                  _resident((1, MEM_HEAD_DIM))],
        out_specs=[row, row], out_shape=[out, out],
        compiler_params=_params(1), name="vmq")(
            h, wv, wm, gm.reshape(1, MEM_HEAD_DIM))


def _conv_body(prev_ref, main_ref, next_ref, w_ref, cb_ref, lg_ref, lb_ref,
               o_ref, xp_ref, y_ref):
    i = pl.program_id(1)
    halo = BF16_SUBLANES
    prev = prev_ref[0].astype(F32)
    nxt = next_ref[0].astype(F32)
    xp_ref[0:halo, :] = jnp.where(i > 0, prev, 0.0)
    xp_ref[halo:halo + TS, :] = main_ref[0].astype(F32)
    xp_ref[halo + TS:, :] = jnp.where(i < pl.num_programs(1) - 1, nxt, 0.0)
    base = halo - CONV_PAD
    for j in range(D_MODEL // LANES):
        lanes = slice(j * LANES, (j + 1) * LANES)
        acc = jnp.zeros((TS, LANES), F32)
        for k in range(CONV_WIDTH):
            acc = acc + xp_ref[base + k:base + k + TS, lanes] * w_ref[k:k + 1, lanes]
        y_ref[:, lanes] = acc
    y = y_ref[...] + cb_ref[...]
    mu = jnp.mean(y, axis=-1, keepdims=True)
    yc = y - mu
    var = jnp.mean(yc * yc, axis=-1, keepdims=True)
    z = yc * lax.rsqrt(var + EPS) * lg_ref[...] + lb_ref[...]
    o_ref[0] = (z * jax.nn.sigmoid(z)).astype(BF16)


def _conv(u, conv_w, conv_b, ln_g, ln_b):
    b, s, c = u.shape
    halo = BF16_SUBLANES
    per = TS // halo
    last = s // halo - 1
    main = pl.BlockSpec((1, TS, c), lambda bi, i: (bi, i, 0))
    prev = pl.BlockSpec((1, halo, c), lambda bi, i: (bi, jnp.maximum(i * per - 1, 0), 0))
    nxt = pl.BlockSpec((1, halo, c), lambda bi, i: (bi, jnp.minimum((i + 1) * per, last), 0))
    return pl.pallas_call(
        _conv_body, grid=(b, s // TS),
        in_specs=[prev, main, nxt, _resident((CONV_WIDTH, c)), _resident((1, c)),
                  _resident((1, c)), _resident((1, c))],
        out_specs=main, out_shape=jax.ShapeDtypeStruct((b, s, c), BF16),
        scratch_shapes=[pltpu.VMEM((TS + 2 * halo, c), F32),
                        pltpu.VMEM((TS, c), F32)],
        compiler_params=_params(2), name="conv")(
            u, u, u, conv_w.reshape(CONV_WIDTH, c), conv_b.reshape(1, c),
            ln_g.reshape(1, c), ln_b.reshape(1, c))


def _diff_body(bound_ref, lam_ref, sg_ref, q_ref, k_ref, v_ref, o_ref):
    lp = lam_ref[...]
    lam = (jnp.exp(jnp.sum(lp[0:1] * lp[1:2], axis=-1, keepdims=True))
           - jnp.exp(jnp.sum(lp[2:3] * lp[3:4], axis=-1, keepdims=True)) + LAM_INIT)
    q = q_ref[0]
    lane = lax.broadcasted_iota(jnp.int32, q.shape, 1)
    zero = jnp.zeros_like(q)
    qq = jnp.concatenate([jnp.where(lane < DIFF_HEAD_DIM, q, zero),
                          jnp.where(lane < DIFF_HEAD_DIM, zero, q)], axis=0)
    bound = bound_ref[0]
    n_keys = k_ref.shape[1]

    def shifted_by_bound():
        o = jnp.zeros((2 * TQ, LANES), F32)
        part = jnp.zeros((2 * TQ, LANES), F32)
        for c in range(n_keys // KEY_CHUNK):
            keys = slice(c * KEY_CHUNK, (c + 1) * KEY_CHUNK)
            e = jnp.exp(_mm_nt(qq, k_ref[0, keys, :]) - bound)
            for j in range(KEY_CHUNK // LANES):
                part = part + e[:, j * LANES:(j + 1) * LANES]
            o = o + _mm(e.astype(BF16), v_ref[0, keys, :])
        return o / jnp.sum(part, axis=-1, keepdims=True)

    def shifted_by_row_max():
        s = _mm_nt(qq, k_ref[0])
        e = jnp.exp(s - jnp.max(s, axis=-1, keepdims=True))
        l = jnp.sum(e, axis=-1, keepdims=True)
        return _mm(e.astype(BF16), v_ref[0]) / l

    o = lax.cond(bound <= MAX_SHIFT_BOUND, shifted_by_bound, shifted_by_row_max)
    od = o[:TQ] - lam * o[TQ:]
    o_ref[0] = (_rms(od, sg_ref[...]) * (1.0 - LAM_INIT)).astype(BF16)


def _diff_attention(q, k, v, diff_lambda, subln, score_bound):
    b, s, _ = q.shape
    width = 2 * DIFF_HEAD_DIM
    qblk = pl.BlockSpec((1, TQ, width), lambda bi, hi, i: (bi, i, hi))
    kvblk = pl.BlockSpec((1, s, width), lambda bi, hi, i: (bi, 0, hi))
    return pl.pallas_call(
        _diff_body, grid=(b, DIFF_HEADS, s // TQ),
        in_specs=[pl.BlockSpec(memory_space=pltpu.SMEM),
                  _resident((4, DIFF_HEAD_DIM)), _resident((1, width)),
                  qblk, kvblk, kvblk],
        out_specs=qblk, out_shape=jax.ShapeDtypeStruct(q.shape, BF16),
        compiler_params=_params(3), name="diffattn")(
            score_bound.reshape(1), diff_lambda, subln.reshape(1, width), q, k, v)


def _memkv_body(m_ref, g_ref, wk_ref, wv_ref, gk_ref, k_ref, v_ref):
    xn = _rms(m_ref[...], g_ref[...]).astype(BF16)
    v_ref[...] = _mm(xn, wv_ref[...]).astype(BF16)
    y = _mm(xn, wk_ref[...])
    for g in range(MEM_HEADS):
        sl = slice(g * MEM_HEAD_DIM, (g + 1) * MEM_HEAD_DIM)
        k_ref[:, sl] = _rms(y[:, sl], gk_ref[...]).astype(BF16)


def _memkv(mem, gain, wk, wv, gk):
    n = mem.shape[0]
    row = pl.BlockSpec((TM, D_MODEL), lambda i: (i, 0))
    out = jax.ShapeDtypeStruct((n, D_MODEL), BF16)
    return pl.pallas_call(
        _memkv_body, grid=(n // TM,),
        in_specs=[row, _resident((1, D_MODEL)), _resident(wk.shape),
                  _resident(wv.shape), _resident((1, MEM_HEAD_DIM))],
        out_specs=[row, row], out_shape=[out, out],
        compiler_params=_params(1), name="memkv")(
            mem, gain.reshape(1, D_MODEL), wk, wv, gk.reshape(1, MEM_HEAD_DIM))


def _memattn_body(q_ref, k_ref, v_ref, o_ref):
    for g in range(MEM_HEADS):
        sl = slice(g * MEM_HEAD_DIM, (g + 1) * MEM_HEAD_DIM)
        s = _mm_nt(q_ref[0, :, sl], k_ref[0, :, sl])
        e = jnp.exp(s - jnp.max(s, axis=-1, keepdims=True))
        l = jnp.sum(e, axis=-1, keepdims=True)
        o_ref[0, :, sl] = (_mm(e.astype(BF16), v_ref[0, :, sl]) / l).astype(BF16)


def _memattn(q, k, v):
    b, s, w = q.shape
    m = k.shape[1]
    qblk = pl.BlockSpec((1, TQ_MEM, w), lambda bi, i: (bi, i, 0))
    kvblk = pl.BlockSpec((1, m, w), lambda bi, i: (bi, 0, 0))
    return pl.pallas_call(
        _memattn_body, grid=(b, s // TQ_MEM),
        in_specs=[qblk, kvblk, kvblk], out_specs=qblk,
        out_shape=jax.ShapeDtypeStruct(q.shape, BF16),
        compiler_params=_params(2), name="memattn")(q, k, v)


def _merge_body(x_ref, h_ref, c_ref, d_ref, m_ref, wc_ref, wd_ref, wm_ref,
                wg_ref, bg_ref, wo_ref, o_ref):
    gates = jax.nn.sigmoid(_mm(h_ref[...], wg_ref[...]) + bg_ref[...])
    merged = (gates[:, :D_MODEL] * _mm(c_ref[...], wc_ref[...])
              + gates[:, D_MODEL:2 * D_MODEL] * _mm(d_ref[...], wd_ref[...])
              + gates[:, 2 * D_MODEL:] * _mm(m_ref[...], wm_ref[...]))
    o_ref[...] = x_ref[...] + _mm(merged.astype(BF16), wo_ref[...])


def _merge(x1, h, c, d, m, wc, wd, wm, wg, bg, wo):
    n = x1.shape[0]
    row = pl.BlockSpec((TM, D_MODEL), lambda i: (i, 0))
    return pl.pallas_call(
        _merge_body, grid=(n // TM,),
        in_specs=[row, row, row, row, row, _resident(wc.shape), _resident(wd.shape),
                  _resident(wm.shape), _resident(wg.shape),
                  _resident((1, 3 * D_MODEL)), _resident(wo.shape)],
        out_specs=row, out_shape=jax.ShapeDtypeStruct((n, D_MODEL), F32),
        compiler_params=_params(1), name="merge")(
            x1, h, c, d, m, wc, wd, wm, wg, bg.reshape(1, 3 * D_MODEL), wo)


def kernel(x, mem, ffn1_norm, ffn1_w_gu, ffn1_w_down, mix_norm, mem_norm, w_in, b_gate,
           conv_w, conv_b, conv_ln_g, conv_ln_b, w_conv_out, diff_q_norm, diff_k_norm,
           diff_lambda, diff_subln, w_diff_out, w_mem_kv, mem_q_norm, mem_k_norm,
           w_mem_out, w_o, ffn2_norm, ffn2_w_gu, ffn2_w_down):
    b, s, d = x.shape
    m = mem.shape[1]
    n = b * s
    assert x.shape[2] == D_MODEL and ffn1_norm.shape[0] == 1
    assert n % TM == 0 and s % TM == 0 and s % TQ == 0 and s % TS == 0
    assert (b * m) % TM == 0 and s % TQ_MEM == 0

    w_in_b = w_in[0].astype(BF16)
    cols = [w_in_b[:, i * D_MODEL:(i + 1) * D_MODEL] for i in range(6)]
    w_gate = w_in_b[:, 6 * D_MODEL:]

    x1, h = _ffn(x.reshape(n, d), ffn1_norm[0], ffn1_w_gu[0], ffn1_w_down[0],
                 next_gain=mix_norm[0])

    u = _glu(h, cols[0], cols[1])
    cact = _conv(u.reshape(b, s, d), conv_w[0], conv_b[0], conv_ln_g[0], conv_ln_b[0])

    q, k = _qk(h, cols[2], cols[3], diff_q_norm[0], diff_k_norm[0], s)
    v, mq = _vmq(h, cols[4], cols[5], mem_q_norm[0])
    score_bound = (1.01 * DIFF_HEAD_DIM ** 0.5 * jnp.max(jnp.abs(diff_q_norm[0]))
                   * jnp.max(jnp.abs(diff_k_norm[0])))
    od = _diff_attention(q.reshape(b, s, d), k.reshape(b, s, d), v.reshape(b, s, d),
                         diff_lambda[0], diff_subln[0], score_bound)

    w_kv = w_mem_kv[0].astype(BF16)
    mk, mv = _memkv(mem.reshape(b * m, d), mem_norm[0], w_kv[:, :D_MODEL],
                    w_kv[:, D_MODEL:], mem_k_norm[0])
    om = _memattn(mq.reshape(b, s, d), mk.reshape(b, m, d), mv.reshape(b, m, d))

    x2 = _merge(x1, h, cact.reshape(n, d), od.reshape(n, d), om.reshape(n, d),
                w_conv_out[0].astype(BF16), w_diff_out[0].astype(BF16),
                w_mem_out[0].astype(BF16), w_gate, b_gate[0], w_o[0].astype(BF16))

    x3 = _ffn(x2, ffn2_norm[0], ffn2_w_gu[0], ffn2_w_down[0])
    return x3.reshape(b, s, d)
```

```python
import functools
import math

import jax
import jax.numpy as jnp
from jax import lax
from jax.experimental import pallas as pl
from jax.experimental.pallas import tpu as pltpu

F32 = jnp.float32
BF16 = jnp.bfloat16

D_MODEL = 1024
D_FF = 2816
CONV_WIDTH = 31
CONV_PAD = CONV_WIDTH // 2
DIFF_HEADS = 8
DIFF_HEAD_DIM = 64
MEM_HEADS = 4
MEM_HEAD_DIM = 256
ROPE_THETA = 10000.0
EPS = 1e-6
LAM_INIT = 0.8 - 0.6 * math.exp(-0.3 * 0)
LOG2E = math.log2(math.e)
Q_SCALE = DIFF_HEAD_DIM ** -0.5 * LOG2E

LANES = 128
SUBLANES = 8
BF16_SUBLANES = 16
MXU_DIM = 256
FF_CHUNK = MXU_DIM
N_FF_CHUNKS = D_FF // FF_CHUNK
VMEM_LIMIT = 56 * 1024 * 1024

TM = 512
TMX = 512
TQ = 512
Q_GROUP = MXU_DIM // 2
KEY_CHUNK = 2 * MXU_DIM
TQ_MEM = 512
CONV_ROWS = 128
HALO = BF16_SUBLANES
MAX_SHIFT_BOUND = 60.0


def _params(n_axes):
    return pltpu.CompilerParams(
        dimension_semantics=("arbitrary",) * n_axes,
        vmem_limit_bytes=VMEM_LIMIT)


def _resident(shape):
    zeros = (0,) * len(shape)
    return pl.BlockSpec(shape, lambda *_: zeros, pipeline_mode=pl.Buffered(1))


def _rms(x, gain):
    return x * lax.rsqrt(jnp.mean(x * x, axis=-1, keepdims=True) + EPS) * gain


def _mm(a, b):
    return jnp.dot(a, b, preferred_element_type=F32)


def _mm_nt(a, b):
    return lax.dot_general(a, b, (((1,), (1,)), ((), ())),
                           preferred_element_type=F32)


def _ffn_body(emit_h, x_ref, g_ref, wg_ref, wu_ref, wd_ref, *rest):
    if emit_h:
        g2_ref, o_ref, h_ref, a_ref = rest
    else:
        o_ref, a_ref = rest
    x = x_ref[...]
    xn = _rms(x, g_ref[...]).astype(BF16)
    for c in range(N_FF_CHUNKS):
        gate = _mm(xn, wg_ref[c])
        up = _mm(xn, wu_ref[c])
        a_ref[:, c * FF_CHUNK:(c + 1) * FF_CHUNK] = (
            gate * jax.nn.sigmoid(gate) * up).astype(BF16)
    x1 = x + 0.5 * _mm(a_ref[...], wd_ref[...])
    o_ref[...] = x1
    if emit_h:
        h_ref[...] = _rms(x1, g2_ref[...]).astype(BF16)


def _ffn(x, gain, w_gu, w_down, next_gain=None):
    n = x.shape[0]
    emit_h = next_gain is not None
    wg = w_gu[:, :D_FF].reshape(D_MODEL, N_FF_CHUNKS, FF_CHUNK)
    wu = w_gu[:, D_FF:].reshape(D_MODEL, N_FF_CHUNKS, FF_CHUNK)
    wg = wg.transpose(1, 0, 2).astype(BF16)
    wu = wu.transpose(1, 0, 2).astype(BF16)
    wd = w_down.astype(BF16)
    row = pl.BlockSpec((TM, D_MODEL), lambda i: (i, 0))
    in_specs = [row, _resident((1, D_MODEL)), _resident(wg.shape),
                _resident(wu.shape), _resident(wd.shape)]
    args = [x, gain.reshape(1, D_MODEL), wg, wu, wd]
    out_shape = [jax.ShapeDtypeStruct((n, D_MODEL), F32)]
    out_specs = [row]
    if emit_h:
        in_specs.append(_resident((1, D_MODEL)))
        args.append(next_gain.reshape(1, D_MODEL))
        out_shape.append(jax.ShapeDtypeStruct((n, D_MODEL), BF16))
        out_specs.append(row)
    out = pl.pallas_call(
        functools.partial(_ffn_body, emit_h),
        grid=(n // TM,),
        in_specs=in_specs, out_specs=out_specs, out_shape=out_shape,
        scratch_shapes=[pltpu.VMEM((TM, D_FF), BF16)],
        compiler_params=_params(1), name="ffn")(*args)
    return out if emit_h else out[0]


def _head_norm_rope(y, gain, group_mean, cos, sin_signed, scale):
    lane = lax.broadcasted_iota(jnp.int32, (y.shape[0], LANES), 1)
    upper_half = (lane & (DIFF_HEAD_DIM // 2)) != 0
    outs = []
    width = group_mean.shape[0]
    for g in range(y.shape[1] // width):
        yg = y[:, g * width:(g + 1) * width]
        ms = _mm((yg * yg).astype(BF16), group_mean)
        yn = yg * lax.rsqrt(ms + EPS) * gain[:, g * width:(g + 1) * width]
        for j in range(width // LANES):
            t = yn[:, j * LANES:(j + 1) * LANES]
            swapped = jnp.where(upper_half,
                                pltpu.roll(t, DIFF_HEAD_DIM // 2, 1),
                                pltpu.roll(t, LANES - DIFF_HEAD_DIM // 2, 1))
            outs.append(((t * cos + swapped * sin_signed) * scale).astype(BF16))
    return jnp.concatenate(outs, axis=1)


def _qk_body(h_ref, wq_ref, wk_ref, gq_ref, gk_ref, gm_ref, cos_ref, sin_ref,
             q_ref, k_ref):
    h = h_ref[...]
    cos = cos_ref[...]
    sin_signed = sin_ref[...]
    gm = gm_ref[...]
    q_ref[...] = _head_norm_rope(_mm(h, wq_ref[...]), gq_ref[...], gm, cos,
                                 sin_signed, Q_SCALE)
    k_ref[...] = _head_norm_rope(_mm(h, wk_ref[...]), gk_ref[...], gm, cos,
                                 sin_signed, 1.0)


def _qk(h, wq, wk, gq, gk, seq):
    n = h.shape[0]
    half = DIFF_HEAD_DIM // 2
    inv_freq = 1.0 / (ROPE_THETA ** (jnp.arange(0, DIFF_HEAD_DIM, 2, dtype=F32)
                                     / DIFF_HEAD_DIM))
    ang = jnp.arange(seq, dtype=F32)[:, None] * inv_freq[None, :]
    cos, sin = jnp.cos(ang), jnp.sin(ang)
    cos_t = jnp.tile(cos, (1, LANES // half))
    sin_t = jnp.tile(jnp.concatenate([-sin, sin], axis=1), (1, LANES // DIFF_HEAD_DIM))
    ids = jnp.arange(MXU_DIM) // DIFF_HEAD_DIM
    group_mean = jnp.where(ids[:, None] == ids[None, :], 1.0 / DIFF_HEAD_DIM,
                           0.0).astype(BF16)
    reps = D_MODEL // DIFF_HEAD_DIM
    row = pl.BlockSpec((TM, D_MODEL), lambda i: (i, 0))
    tab = pl.BlockSpec((TM, LANES), lambda i: (i % (seq // TM), 0))
    out = jax.ShapeDtypeStruct((n, D_MODEL), BF16)
    return pl.pallas_call(
        _qk_body, grid=(n // TM,),
        in_specs=[row, _resident(wq.shape), _resident(wk.shape),
                  _resident((1, D_MODEL)), _resident((1, D_MODEL)),
                  _resident(group_mean.shape), tab, tab],
        out_specs=[row, row], out_shape=[out, out],
        compiler_params=_params(1), name="qk")(
            h, wq, wk, jnp.tile(gq, reps).reshape(1, D_MODEL),
            jnp.tile(gk, reps).reshape(1, D_MODEL), group_mean, cos_t, sin_t)


def _vmq_body(h_ref, wvt_ref, wm_ref, gm_ref, vt_ref, mq_ref):
    h = h_ref[0]
    vt_ref[0] = _mm_nt(wvt_ref[...], h).astype(BF16)
    y = _mm(h, wm_ref[...])
    scale = MEM_HEAD_DIM ** -0.5
    for g in range(MEM_HEADS):
        sl = slice(g * MEM_HEAD_DIM, (g + 1) * MEM_HEAD_DIM)
        mq_ref[0, :, sl] = (_rms(y[:, sl], gm_ref[...]) * scale).astype(BF16)


def _vmq(h, wvt, wm, gm):
    b, s, d = h.shape
    row = pl.BlockSpec((1, TM, d), lambda bi, i: (bi, i, 0))
    col = pl.BlockSpec((1, d, TM), lambda bi, i: (bi, 0, i))
    return pl.pallas_call(
        _vmq_body, grid=(b, s // TM),
        in_specs=[row, _resident(wvt.shape), _resident(wm.shape),
                  _resident((1, MEM_HEAD_DIM))],
        out_specs=[col, row],
        out_shape=[jax.ShapeDtypeStruct((b, d, s), BF16),
                   jax.ShapeDtypeStruct((b, s, d), BF16)],
        compiler_params=_params(2), name="vmq")(
            h, wvt, wm, gm.reshape(1, MEM_HEAD_DIM))


def _diff_body(bound_ref, lam_ref, sg_ref, q_ref, k_ref, vt_ref, o_ref):
    lp = lam_ref[...]
    lam = (jnp.exp(jnp.sum(lp[0:1] * lp[1:2], axis=-1, keepdims=True))
           - jnp.exp(jnp.sum(lp[2:3] * lp[3:4], axis=-1, keepdims=True)) + LAM_INIT)
    bound = bound_ref[0]
    n_keys = k_ref.shape[1]
    lane = lax.broadcasted_iota(jnp.int32, (Q_GROUP, LANES), 1)
    first = lane < DIFF_HEAD_DIM

    def shifted_by_bound(qq):
        ot = jnp.zeros((LANES, MXU_DIM), F32)
        part = jnp.zeros((SUBLANES, MXU_DIM), F32)
        for c in range(n_keys // KEY_CHUNK):
            keys = slice(c * KEY_CHUNK, (c + 1) * KEY_CHUNK)
            et = jnp.exp2(_mm_nt(k_ref[0, keys, :], qq) - bound)
            part = part + jnp.sum(
                et.reshape(KEY_CHUNK // SUBLANES, SUBLANES, MXU_DIM), axis=0)
            ot = ot + _mm(vt_ref[0, :, keys], et.astype(BF16))
        return ot / jnp.sum(part, axis=0, keepdims=True)

    def shifted_by_max(qq):
        st = _mm_nt(k_ref[0], qq)
        et = jnp.exp2(st - jnp.max(st, axis=0, keepdims=True))
        return _mm(vt_ref[0], et.astype(BF16)) / jnp.sum(et, axis=0, keepdims=True)

    def attend(softmax_pv):
        for g in range(TQ // Q_GROUP):
            rows = slice(g * Q_GROUP, (g + 1) * Q_GROUP)
            q = q_ref[0, rows, :]
            zero = jnp.zeros_like(q)
            qq = jnp.concatenate([jnp.where(first, q, zero), jnp.where(first, zero, q)],
                                 axis=0)
            ot = softmax_pv(qq)
            odt = ot[:, :Q_GROUP] - lam * ot[:, Q_GROUP:]
            ms = jnp.mean(odt * odt, axis=0, keepdims=True)
            y = odt * lax.rsqrt(ms + EPS) * sg_ref[...] * (1.0 - LAM_INIT)
            o_ref[0, rows, :] = y.T.astype(BF16)

    use_bound = bound <= MAX_SHIFT_BOUND

    @pl.when(use_bound)
    def _():
        attend(shifted_by_bound)

    @pl.when(jnp.logical_not(use_bound))
    def _():
        attend(shifted_by_max)


def _diff_attention(q, k, vt, diff_lambda, subln, score_bound):
    b, s, _ = q.shape
    width = 2 * DIFF_HEAD_DIM
    qblk = pl.BlockSpec((1, TQ, width), lambda bi, hi, i: (bi, i, hi))
    kblk = pl.BlockSpec((1, s, width), lambda bi, hi, i: (bi, 0, hi))
    vblk = pl.BlockSpec((1, width, s), lambda bi, hi, i: (bi, hi, 0))
    gain = jnp.broadcast_to(subln.reshape(width, 1), (width, Q_GROUP))
    return pl.pallas_call(
        _diff_body, grid=(b, DIFF_HEADS, s // TQ),
        in_specs=[pl.BlockSpec(memory_space=pltpu.SMEM),
                  _resident((4, DIFF_HEAD_DIM)), _resident((width, Q_GROUP)),
                  qblk, kblk, vblk],
        out_specs=qblk, out_shape=jax.ShapeDtypeStruct(q.shape, BF16),
        compiler_params=_params(3), name="diffattn")(
            score_bound.reshape(1), diff_lambda, gain, q, k, vt)


def _memkv_body(m_ref, g_ref, wk_ref, wv_ref, gk_ref, k_ref, v_ref):
    xn = _rms(m_ref[...], g_ref[...]).astype(BF16)
    v_ref[...] = _mm(xn, wv_ref[...]).astype(BF16)
    y = _mm(xn, wk_ref[...])
    for g in range(MEM_HEADS):
        sl = slice(g * MEM_HEAD_DIM, (g + 1) * MEM_HEAD_DIM)
        k_ref[:, sl] = _rms(y[:, sl], gk_ref[...]).astype(BF16)


def _memkv(mem, gain, wk, wv, gk):
    n = mem.shape[0]
    row = pl.BlockSpec((TM, D_MODEL), lambda i: (i, 0))
    out = jax.ShapeDtypeStruct((n, D_MODEL), BF16)
    return pl.pallas_call(
        _memkv_body, grid=(n // TM,),
        in_specs=[row, _resident((1, D_MODEL)), _resident(wk.shape),
                  _resident(wv.shape), _resident((1, MEM_HEAD_DIM))],
        out_specs=[row, row], out_shape=[out, out],
        compiler_params=_params(1), name="memkv")(
            mem, gain.reshape(1, D_MODEL), wk, wv, gk.reshape(1, MEM_HEAD_DIM))


def _memattn_body(q_ref, k_ref, v_ref, o_ref):
    for g in range(MEM_HEADS):
        sl = slice(g * MEM_HEAD_DIM, (g + 1) * MEM_HEAD_DIM)
        s = _mm_nt(q_ref[0, :, sl], k_ref[0, :, sl])
        e = jnp.exp(s - jnp.max(s, axis=-1, keepdims=True))
        l = jnp.sum(e, axis=-1, keepdims=True)
        o_ref[0, :, sl] = (_mm(e.astype(BF16), v_ref[0, :, sl]) / l).astype(BF16)


def _memattn(q, k, v):
    b, s, w = q.shape
    m = k.shape[1]
    qblk = pl.BlockSpec((1, TQ_MEM, w), lambda bi, i: (bi, i, 0))
    kvblk = pl.BlockSpec((1, m, w), lambda bi, i: (bi, 0, 0))
    return pl.pallas_call(
        _memattn_body, grid=(b, s // TQ_MEM),
        in_specs=[qblk, kvblk, kvblk], out_specs=qblk,
        out_shape=jax.ShapeDtypeStruct(q.shape, BF16),
        compiler_params=_params(2), name="memattn")(q, k, v)


def _mix_body(x_ref, hp_ref, h_ref, hn_ref, d_ref, m_ref, wa_ref, wb_ref, cw_ref,
              cb_ref, lg_ref, lb_ref, wc_ref, wd_ref, wm_ref, wg_ref, bg_ref, wo_ref,
              o_ref, xp_ref, xs_ref, y_ref):
    i = pl.program_id(1)
    rows = TMX + 2 * HALO
    h = h_ref[0]
    hcat = jnp.concatenate([hp_ref[0], h, hn_ref[0]], axis=0)
    u = _mm(hcat, wa_ref[...]) * jax.nn.sigmoid(_mm(hcat, wb_ref[...]))
    r = lax.broadcasted_iota(jnp.int32, (rows, 1), 0)
    lo = jnp.where(i > 0, 0, HALO)
    hi = jnp.where(i < pl.num_programs(1) - 1, rows, HALO + TMX)
    xp_ref[...] = jnp.where((r >= lo) & (r < hi), u, 0.0)

    base = HALO - CONV_PAD
    span = rows - SUBLANES

    for j in range(D_MODEL // LANES):
        lanes = slice(j * LANES, (j + 1) * LANES)
        for p in range(1, SUBLANES):
            xs_ref[p - 1, 0:span, :] = xp_ref[p:p + span, lanes]
        for r0 in range(0, TMX, CONV_ROWS):
            acc = jnp.zeros((CONV_ROWS, LANES), F32)
            for k in range(CONV_WIDTH):
                p = (base + k) % SUBLANES
                start = base + k - p + r0
                if p == 0:
                    win = xp_ref[start:start + CONV_ROWS, lanes]
                else:
                    win = xs_ref[p - 1, start:start + CONV_ROWS, :]
                acc = acc + win * cw_ref[k:k + 1, lanes]
            y_ref[r0:r0 + CONV_ROWS, lanes] = acc
    y = y_ref[...] + cb_ref[...]
    mu = jnp.mean(y, axis=-1, keepdims=True)
    yc = y - mu
    var = jnp.mean(yc * yc, axis=-1, keepdims=True)
    z = yc * lax.rsqrt(var + EPS) * lg_ref[...] + lb_ref[...]
    c = (z * jax.nn.sigmoid(z)).astype(BF16)

    gates = jax.nn.sigmoid(_mm(h, wg_ref[...]) + bg_ref[...])
    merged = (gates[:, :D_MODEL] * _mm(c, wc_ref[...])
              + gates[:, D_MODEL:2 * D_MODEL] * _mm(d_ref[0], wd_ref[...])
              + gates[:, 2 * D_MODEL:] * _mm(m_ref[0], wm_ref[...]))
    o_ref[0] = x_ref[0] + _mm(merged.astype(BF16), wo_ref[...])


def _mix(x1, h, od, om, wa, wb, conv_w, conv_b, ln_g, ln_b, wc, wd, wm, wg, bg, wo):
    b, s, d = x1.shape
    per = TMX // HALO
    last = s // HALO - 1
    row = pl.BlockSpec((1, TMX, d), lambda bi, i: (bi, i, 0))
    prev = pl.BlockSpec((1, HALO, d), lambda bi, i: (bi, jnp.maximum(i * per - 1, 0), 0))
    nxt = pl.BlockSpec((1, HALO, d), lambda bi, i: (bi, jnp.minimum((i + 1) * per, last), 0))
    vec = _resident((1, d))
    return pl.pallas_call(
        _mix_body, grid=(b, s // TMX),
        in_specs=[row, prev, row, nxt, row, row, _resident(wa.shape), _resident(wb.shape),
                  _resident((CONV_WIDTH, d)), vec, vec, vec, _resident(wc.shape),
                  _resident(wd.shape), _resident(wm.shape), _resident(wg.shape),
                  _resident((1, 3 * d)), _resident(wo.shape)],
        out_specs=row, out_shape=jax.ShapeDtypeStruct((b, s, d), F32),
        scratch_shapes=[pltpu.VMEM((TMX + 2 * HALO, d), F32),
                        pltpu.VMEM((SUBLANES - 1, TMX + 2 * HALO, LANES), F32),
                        pltpu.VMEM((TMX, d), F32)],
        compiler_params=_params(2), name="mix")(
            x1, h, h, h, od, om, wa, wb, conv_w.reshape(CONV_WIDTH, d),
            conv_b.reshape(1, d), ln_g.reshape(1, d), ln_b.reshape(1, d),
            wc, wd, wm, wg, bg.reshape(1, 3 * d), wo)


def kernel(x, mem, ffn1_norm, ffn1_w_gu, ffn1_w_down, mix_norm, mem_norm, w_in, b_gate,
           conv_w, conv_b, conv_ln_g, conv_ln_b, w_conv_out, diff_q_norm, diff_k_norm,
           diff_lambda, diff_subln, w_diff_out, w_mem_kv, mem_q_norm, mem_k_norm,
           w_mem_out, w_o, ffn2_norm, ffn2_w_gu, ffn2_w_down):
    b, s, d = x.shape
    m = mem.shape[1]
    n = b * s
    assert d == D_MODEL and ffn1_norm.shape[0] == 1
    assert s % TM == 0 and s % TQ == 0 and s % TQ_MEM == 0 and s % KEY_CHUNK == 0
    assert (b * m) % TM == 0 and TMX % CONV_ROWS == 0 and HALO >= CONV_PAD
    assert s % TMX == 0

    w_in_b = w_in[0].astype(BF16)
    cols = [w_in_b[:, i * D_MODEL:(i + 1) * D_MODEL] for i in range(6)]
    w_gate = w_in_b[:, 6 * D_MODEL:]

    x1, h = _ffn(x.reshape(n, d), ffn1_norm[0], ffn1_w_gu[0], ffn1_w_down[0],
                 next_gain=mix_norm[0])
    h3 = h.reshape(b, s, d)

    q, k = _qk(h, cols[2], cols[3], diff_q_norm[0], diff_k_norm[0], s)
    vt, mq = _vmq(h3, cols[4].T, cols[5], mem_q_norm[0])
    score_bound = (1.01 * LOG2E * DIFF_HEAD_DIM ** 0.5
                   * jnp.max(jnp.abs(diff_q_norm[0])) * jnp.max(jnp.abs(diff_k_norm[0])))
    od = _diff_attention(q.reshape(b, s, d), k.reshape(b, s, d), vt,
                         diff_lambda[0], diff_subln[0], score_bound)

    w_kv = w_mem_kv[0].astype(BF16)
    mk, mv = _memkv(mem.reshape(b * m, d), mem_norm[0], w_kv[:, :D_MODEL],
                    w_kv[:, D_MODEL:], mem_k_norm[0])
    om = _memattn(mq, mk.reshape(b, m, d), mv.reshape(b, m, d))

    x2 = _mix(x1.reshape(b, s, d), h3, od, om, cols[0], cols[1], conv_w[0], conv_b[0],
              conv_ln_g[0], conv_ln_b[0], w_conv_out[0].astype(BF16),
              w_diff_out[0].astype(BF16), w_mem_out[0].astype(BF16), w_gate, b_gate[0],
              w_o[0].astype(BF16))

    x3 = _ffn(x2.reshape(n, d), ffn2_norm[0], ffn2_w_gu[0], ffn2_w_down[0])
    return x3.reshape(b, s, d)
```

```python
import functools
import math

import jax
import jax.numpy as jnp
from jax import lax
from jax.experimental import pallas as pl
from jax.experimental.pallas import tpu as pltpu

F32 = jnp.float32
BF16 = jnp.bfloat16

D_MODEL = 1024
D_FF = 2816
CONV_WIDTH = 31
CONV_PAD = CONV_WIDTH // 2
DIFF_HEADS = 8
DIFF_HEAD_DIM = 64
MEM_HEADS = 4
MEM_HEAD_DIM = 256
ROPE_THETA = 10000.0
EPS = 1e-6
LAM_INIT = 0.8 - 0.6 * math.exp(-0.3 * 0)
LOG2E = math.log2(math.e)
Q_SCALE = DIFF_HEAD_DIM ** -0.5 * LOG2E

LANES = 128
SUBLANES = 8
BF16_SUBLANES = 16
MXU_DIM = 256
FF_CHUNK = MXU_DIM
N_FF_CHUNKS = D_FF // FF_CHUNK
VMEM_LIMIT = 56 * 1024 * 1024

TM = 512
TMX = 512
TQ = 2048
Q_GROUP = MXU_DIM // 2
KEY_CHUNK = 2 * MXU_DIM
TQ_MEM = 512
CONV_ROWS = 128
HALO = BF16_SUBLANES
MAX_SHIFT_BOUND = 60.0


def _params(n_axes):
    return pltpu.CompilerParams(
        dimension_semantics=("arbitrary",) * n_axes,
        vmem_limit_bytes=VMEM_LIMIT)


def _resident(shape):
    zeros = (0,) * len(shape)
    return pl.BlockSpec(shape, lambda *_: zeros, pipeline_mode=pl.Buffered(1))


def _rms(x, gain):
    return x * lax.rsqrt(jnp.mean(x * x, axis=-1, keepdims=True) + EPS) * gain


def _mm(a, b):
    return jnp.dot(a, b, preferred_element_type=F32)


def _mm_nt(a, b):
    return lax.dot_general(a, b, (((1,), (1,)), ((), ())),
                           preferred_element_type=F32)


def _ffn_body(emit_h, x_ref, g_ref, wg_ref, wu_ref, wd_ref, *rest):
    if emit_h:
        g2_ref, o_ref, h_ref, a_ref = rest
    else:
        o_ref, a_ref = rest
    x = x_ref[...]
    xn = _rms(x, g_ref[...]).astype(BF16)
    for c in range(N_FF_CHUNKS):
        gate = _mm(xn, wg_ref[c])
        up = _mm(xn, wu_ref[c])
        a_ref[:, c * FF_CHUNK:(c + 1) * FF_CHUNK] = (
            gate * jax.nn.sigmoid(gate) * up).astype(BF16)
    x1 = x + 0.5 * _mm(a_ref[...], wd_ref[...])
    o_ref[...] = x1
    if emit_h:
        h_ref[...] = _rms(x1, g2_ref[...]).astype(BF16)


def _ffn(x, gain, w_gu, w_down, next_gain=None):
    n = x.shape[0]
    emit_h = next_gain is not None
    wg = w_gu[:, :D_FF].reshape(D_MODEL, N_FF_CHUNKS, FF_CHUNK)
    wu = w_gu[:, D_FF:].reshape(D_MODEL, N_FF_CHUNKS, FF_CHUNK)
    wg = wg.transpose(1, 0, 2).astype(BF16)
    wu = wu.transpose(1, 0, 2).astype(BF16)
    wd = w_down.astype(BF16)
    row = pl.BlockSpec((TM, D_MODEL), lambda i: (i, 0))
    in_specs = [row, _resident((1, D_MODEL)), _resident(wg.shape),
                _resident(wu.shape), _resident(wd.shape)]
    args = [x, gain.reshape(1, D_MODEL), wg, wu, wd]
    out_shape = [jax.ShapeDtypeStruct((n, D_MODEL), F32)]
    out_specs = [row]
    if emit_h:
        in_specs.append(_resident((1, D_MODEL)))
        args.append(next_gain.reshape(1, D_MODEL))
        out_shape.append(jax.ShapeDtypeStruct((n, D_MODEL), BF16))
        out_specs.append(row)
    out = pl.pallas_call(
        functools.partial(_ffn_body, emit_h),
        grid=(n // TM,),
        in_specs=in_specs, out_specs=out_specs, out_shape=out_shape,
        scratch_shapes=[pltpu.VMEM((TM, D_FF), BF16)],
        compiler_params=_params(1), name="ffn")(*args)
    return out if emit_h else out[0]


def _head_norm_rope(y, group_mean, direct, crossed):
    lane = lax.broadcasted_iota(jnp.int32, (y.shape[0], LANES), 1)
    upper_half = (lane & (DIFF_HEAD_DIM // 2)) != 0
    outs = []
    width = group_mean.shape[0]
    for g in range(y.shape[1] // width):
        yg = y[:, g * width:(g + 1) * width]
        ms = _mm((yg * yg).astype(BF16), group_mean)
        yn = yg * lax.rsqrt(ms + EPS)
        for j in range(width // LANES):
            t = yn[:, j * LANES:(j + 1) * LANES]
            swapped = jnp.where(upper_half,
                                pltpu.roll(t, DIFF_HEAD_DIM // 2, 1),
                                pltpu.roll(t, LANES - DIFF_HEAD_DIM // 2, 1))
            outs.append((t * direct + swapped * crossed).astype(BF16))
    return jnp.concatenate(outs, axis=1)


def _qk_body(h_ref, wq_ref, wk_ref, gm_ref, qd_ref, qc_ref, kd_ref, kc_ref,
             q_ref, k_ref):
    h = h_ref[...]
    gm = gm_ref[...]
    q_ref[...] = _head_norm_rope(_mm(h, wq_ref[...]), gm, qd_ref[...], qc_ref[...])
    k_ref[...] = _head_norm_rope(_mm(h, wk_ref[...]), gm, kd_ref[...], kc_ref[...])


def _rope_tables(gain, scale, cos, sin):
    half = DIFF_HEAD_DIM // 2
    reps = LANES // DIFF_HEAD_DIM
    swapped_gain = jnp.concatenate([gain[half:], gain[:half]])
    direct = jnp.tile(jnp.concatenate([cos, cos], axis=1) * (gain * scale), (1, reps))
    crossed = jnp.tile(jnp.concatenate([-sin, sin], axis=1) * (swapped_gain * scale),
                       (1, reps))
    return direct, crossed


def _qk(h, wq, wk, gq, gk, seq):
    n = h.shape[0]
    inv_freq = 1.0 / (ROPE_THETA ** (jnp.arange(0, DIFF_HEAD_DIM, 2, dtype=F32)
                                     / DIFF_HEAD_DIM))
    ang = jnp.arange(seq, dtype=F32)[:, None] * inv_freq[None, :]
    cos, sin = jnp.cos(ang), jnp.sin(ang)
    qd, qc = _rope_tables(gq, Q_SCALE, cos, sin)
    kd, kc = _rope_tables(gk, 1.0, cos, sin)
    ids = jnp.arange(MXU_DIM) // DIFF_HEAD_DIM
    group_mean = jnp.where(ids[:, None] == ids[None, :], 1.0 / DIFF_HEAD_DIM,
                           0.0).astype(BF16)
    row = pl.BlockSpec((TM, D_MODEL), lambda i: (i, 0))
    tab = pl.BlockSpec((TM, LANES), lambda i: (i % (seq // TM), 0))
    out = jax.ShapeDtypeStruct((n, D_MODEL), BF16)
    return pl.pallas_call(
        _qk_body, grid=(n // TM,),
        in_specs=[row, _resident(wq.shape), _resident(wk.shape),
                  _resident(group_mean.shape), tab, tab, tab, tab],
        out_specs=[row, row], out_shape=[out, out],
        compiler_params=_params(1), name="qk")(
            h, wq, wk, group_mean, qd, qc, kd, kc)


def _vmq_body(h_ref, wvt_ref, wm_ref, gm_ref, vt_ref, mq_ref):
    h = h_ref[0]
    vt_ref[0] = _mm_nt(wvt_ref[...], h).astype(BF16)
    y = _mm(h, wm_ref[...])
    scale = MEM_HEAD_DIM ** -0.5
    for g in range(MEM_HEADS):
        sl = slice(g * MEM_HEAD_DIM, (g + 1) * MEM_HEAD_DIM)
        mq_ref[0, :, sl] = (_rms(y[:, sl], gm_ref[...]) * scale).astype(BF16)


def _vmq(h, wvt, wm, gm):
    b, s, d = h.shape
    row = pl.BlockSpec((1, TM, d), lambda bi, i: (bi, i, 0))
    col = pl.BlockSpec((1, d, TM), lambda bi, i: (bi, 0, i))
    return pl.pallas_call(
        _vmq_body, grid=(b, s // TM),
        in_specs=[row, _resident(wvt.shape), _resident(wm.shape),
                  _resident((1, MEM_HEAD_DIM))],
        out_specs=[col, row],
        out_shape=[jax.ShapeDtypeStruct((b, d, s), BF16),
                   jax.ShapeDtypeStruct((b, s, d), BF16)],
        compiler_params=_params(2), name="vmq")(
            h, wvt, wm, gm.reshape(1, MEM_HEAD_DIM))


def _diff_body(bound_ref, lam_ref, sg_ref, q_ref, k_ref, vt_ref, o_ref):
    lp = lam_ref[...]
    lam = (jnp.exp(jnp.sum(lp[0:1] * lp[1:2], axis=-1, keepdims=True))
           - jnp.exp(jnp.sum(lp[2:3] * lp[3:4], axis=-1, keepdims=True)) + LAM_INIT)
    bound = bound_ref[0]
    n_keys = k_ref.shape[1]
    lane = lax.broadcasted_iota(jnp.int32, (Q_GROUP, LANES), 1)
    first = lane < DIFF_HEAD_DIM

    def shifted_by_bound(qq):
        ot = jnp.zeros((LANES, MXU_DIM), F32)
        part = jnp.zeros((SUBLANES, MXU_DIM), F32)
        for c in range(n_keys // KEY_CHUNK):
            keys = slice(c * KEY_CHUNK, (c + 1) * KEY_CHUNK)
            et = jnp.exp2(_mm_nt(k_ref[0, keys, :], qq) - bound)
            part = part + jnp.sum(
                et.reshape(KEY_CHUNK // SUBLANES, SUBLANES, MXU_DIM), axis=0)
            ot = ot + _mm(vt_ref[0, :, keys], et.astype(BF16))
        return ot / jnp.sum(part, axis=0, keepdims=True)

    def shifted_by_max(qq):
        st = _mm_nt(k_ref[0], qq)
        et = jnp.exp2(st - jnp.max(st, axis=0, keepdims=True))
        return _mm(vt_ref[0], et.astype(BF16)) / jnp.sum(et, axis=0, keepdims=True)

    def attend(softmax_pv):
        for g in range(TQ // Q_GROUP):
            rows = slice(g * Q_GROUP, (g + 1) * Q_GROUP)
            q = q_ref[0, rows, :]
            zero = jnp.zeros_like(q)
            qq = jnp.concatenate([jnp.where(first, q, zero), jnp.where(first, zero, q)],
                                 axis=0)
            ot = softmax_pv(qq)
            odt = ot[:, :Q_GROUP] - lam * ot[:, Q_GROUP:]
            ms = jnp.mean(odt * odt, axis=0, keepdims=True)
            y = odt * lax.rsqrt(ms + EPS) * sg_ref[...] * (1.0 - LAM_INIT)
            o_ref[0, rows, :] = y.T.astype(BF16)

    use_bound = bound <= MAX_SHIFT_BOUND

    @pl.when(use_bound)
    def _():
        attend(shifted_by_bound)

    @pl.when(jnp.logical_not(use_bound))
    def _():
        attend(shifted_by_max)


def _diff_attention(q, k, vt, diff_lambda, subln, score_bound):
    b, s, _ = q.shape
    width = 2 * DIFF_HEAD_DIM
    qblk = pl.BlockSpec((1, TQ, width), lambda bi, hi, i: (bi, i, hi))
    kblk = pl.BlockSpec((1, s, width), lambda bi, hi, i: (bi, 0, hi))
    vblk = pl.BlockSpec((1, width, s), lambda bi, hi, i: (bi, hi, 0))
    gain = jnp.broadcast_to(subln.reshape(width, 1), (width, Q_GROUP))
    return pl.pallas_call(
        _diff_body, grid=(b, DIFF_HEADS, s // TQ),
        in_specs=[pl.BlockSpec(memory_space=pltpu.SMEM),
                  _resident((4, DIFF_HEAD_DIM)), _resident((width, Q_GROUP)),
                  qblk, kblk, vblk],
        out_specs=qblk, out_shape=jax.ShapeDtypeStruct(q.shape, BF16),
        compiler_params=_params(3), name="diffattn")(
            score_bound.reshape(1), diff_lambda, gain, q, k, vt)


def _memkv_body(m_ref, g_ref, wk_ref, wv_ref, gk_ref, k_ref, v_ref):
    xn = _rms(m_ref[...], g_ref[...]).astype(BF16)
    v_ref[...] = _mm(xn, wv_ref[...]).astype(BF16)
    y = _mm(xn, wk_ref[...])
    for g in range(MEM_HEADS):
        sl = slice(g * MEM_HEAD_DIM, (g + 1) * MEM_HEAD_DIM)
        k_ref[:, sl] = _rms(y[:, sl], gk_ref[...]).astype(BF16)


def _memkv(mem, gain, wk, wv, gk):
    n = mem.shape[0]
    row = pl.BlockSpec((TM, D_MODEL), lambda i: (i, 0))
    out = jax.ShapeDtypeStruct((n, D_MODEL), BF16)
    return pl.pallas_call(
        _memkv_body, grid=(n // TM,),
        in_specs=[row, _resident((1, D_MODEL)), _resident(wk.shape),
                  _resident(wv.shape), _resident((1, MEM_HEAD_DIM))],
        out_specs=[row, row], out_shape=[out, out],
        compiler_params=_params(1), name="memkv")(
            mem, gain.reshape(1, D_MODEL), wk, wv, gk.reshape(1, MEM_HEAD_DIM))


def _memattn_body(q_ref, k_ref, v_ref, o_ref):
    for g in range(MEM_HEADS):
        sl = slice(g * MEM_HEAD_DIM, (g + 1) * MEM_HEAD_DIM)
        s = _mm_nt(q_ref[0, :, sl], k_ref[0, :, sl])
        e = jnp.exp(s - jnp.max(s, axis=-1, keepdims=True))
        l = jnp.sum(e, axis=-1, keepdims=True)
        o_ref[0, :, sl] = (_mm(e.astype(BF16), v_ref[0, :, sl]) / l).astype(BF16)


def _memattn(q, k, v):
    b, s, w = q.shape
    m = k.shape[1]
    qblk = pl.BlockSpec((1, TQ_MEM, w), lambda bi, i: (bi, i, 0))
    kvblk = pl.BlockSpec((1, m, w), lambda bi, i: (bi, 0, 0))
    return pl.pallas_call(
        _memattn_body, grid=(b, s // TQ_MEM),
        in_specs=[qblk, kvblk, kvblk], out_specs=qblk,
        out_shape=jax.ShapeDtypeStruct(q.shape, BF16),
        compiler_params=_params(2), name="memattn")(q, k, v)


def _mix_body(tiles_per_seq, hp_ref, hc_ref, hn_ref, x_ref, h_ref, d_ref, m_ref, wa_ref,
              wb_ref, cw_ref, cb_ref, lg_ref, lb_ref, wc_ref, wd_ref, wm_ref, wg_ref,
              bg_ref, wo_ref, o_ref, xp_ref, xs_ref, y_ref):
    t = pl.program_id(0)
    rows = TMX + 2 * HALO

    @pl.when(t == 0)
    def _():
        y_ref[...] = jnp.zeros_like(y_ref)

    y = y_ref[...] + cb_ref[...]

    pos = jnp.minimum(t, pl.num_programs(0) - 2) % tiles_per_seq
    hcat = jnp.concatenate([hp_ref[...], hc_ref[...], hn_ref[...]], axis=0)
    u = _mm(hcat, wa_ref[...]) * jax.nn.sigmoid(_mm(hcat, wb_ref[...]))
    r = lax.broadcasted_iota(jnp.int32, (rows, 1), 0)
    lo = jnp.where(pos > 0, 0, HALO)
    hi = jnp.where(pos < tiles_per_seq - 1, rows, HALO + TMX)
    xp_ref[...] = jnp.where((r >= lo) & (r < hi), u, 0.0)

    base = HALO - CONV_PAD
    span = rows - SUBLANES

    for j in range(D_MODEL // LANES):
        lanes = slice(j * LANES, (j + 1) * LANES)
        for p in range(1, SUBLANES):
            xs_ref[p - 1, 0:span, :] = xp_ref[p:p + span, lanes]
        for r0 in range(0, TMX, CONV_ROWS):
            acc = jnp.zeros((CONV_ROWS, LANES), F32)
            for k in range(CONV_WIDTH):
                p = (base + k) % SUBLANES
                start = base + k - p + r0
                if p == 0:
                    win = xp_ref[start:start + CONV_ROWS, lanes]
                else:
                    win = xs_ref[p - 1, start:start + CONV_ROWS, :]
                acc = acc + win * cw_ref[k:k + 1, lanes]
            y_ref[r0:r0 + CONV_ROWS, lanes] = acc

    mu = jnp.mean(y, axis=-1, keepdims=True)
    yc = y - mu
    var = jnp.mean(yc * yc, axis=-1, keepdims=True)
    z = yc * lax.rsqrt(var + EPS) * lg_ref[...] + lb_ref[...]
    c = (z * jax.nn.sigmoid(z)).astype(BF16)
    gates = jax.nn.sigmoid(_mm(h_ref[...], wg_ref[...]) + bg_ref[...])
    merged = (gates[:, :D_MODEL] * _mm(c, wc_ref[...])
              + gates[:, D_MODEL:2 * D_MODEL] * _mm(d_ref[...], wd_ref[...])
              + gates[:, 2 * D_MODEL:] * _mm(m_ref[...], wm_ref[...]))
    o_ref[...] = x_ref[...] + _mm(merged.astype(BF16), wo_ref[...])


def _mix(x1, h, od, om, seq, wa, wb, conv_w, conv_b, ln_g, ln_b, wc, wd, wm, wg, bg, wo):
    n, d = x1.shape
    tiles = n // TMX
    per = TMX // HALO
    last_halo = n // HALO - 1

    def conv_tile(t):
        return jnp.minimum(t, tiles - 1)

    conv_row = pl.BlockSpec((TMX, d), lambda t: (conv_tile(t), 0))
    prev = pl.BlockSpec((HALO, d), lambda t: (jnp.maximum(conv_tile(t) * per - 1, 0), 0))
    nxt = pl.BlockSpec((HALO, d),
                       lambda t: (jnp.minimum((conv_tile(t) + 1) * per, last_halo), 0))
    row = pl.BlockSpec((TMX, d), lambda t: (jnp.maximum(t - 1, 0), 0))
    vec = _resident((1, d))
    return pl.pallas_call(
        functools.partial(_mix_body, seq // TMX), grid=(tiles + 1,),
        in_specs=[prev, conv_row, nxt, row, row, row, row, _resident(wa.shape),
                  _resident(wb.shape), _resident((CONV_WIDTH, d)), vec, vec, vec,
                  _resident(wc.shape), _resident(wd.shape), _resident(wm.shape),
                  _resident(wg.shape), _resident((1, 3 * d)), _resident(wo.shape)],
        out_specs=row, out_shape=jax.ShapeDtypeStruct((n, d), F32),
        scratch_shapes=[pltpu.VMEM((TMX + 2 * HALO, d), F32),
                        pltpu.VMEM((SUBLANES - 1, TMX + 2 * HALO, LANES), F32),
                        pltpu.VMEM((TMX, d), F32)],
        compiler_params=_params(1), name="mix")(
            h, h, h, x1, h, od, om, wa, wb, conv_w.reshape(CONV_WIDTH, d),
            conv_b.reshape(1, d), ln_g.reshape(1, d), ln_b.reshape(1, d),
            wc, wd, wm, wg, bg.reshape(1, 3 * d), wo)


def kernel(x, mem, ffn1_norm, ffn1_w_gu, ffn1_w_down, mix_norm, mem_norm, w_in, b_gate,
           conv_w, conv_b, conv_ln_g, conv_ln_b, w_conv_out, diff_q_norm, diff_k_norm,
           diff_lambda, diff_subln, w_diff_out, w_mem_kv, mem_q_norm, mem_k_norm,
           w_mem_out, w_o, ffn2_norm, ffn2_w_gu, ffn2_w_down):
    b, s, d = x.shape
    m = mem.shape[1]
    n = b * s
    assert d == D_MODEL and ffn1_norm.shape[0] == 1
    assert s % TM == 0 and s % TQ == 0 and s % TQ_MEM == 0 and s % KEY_CHUNK == 0
    assert (b * m) % TM == 0 and TMX % CONV_ROWS == 0 and HALO >= CONV_PAD
    assert s % TMX == 0

    w_in_b = w_in[0].astype(BF16)
    cols = [w_in_b[:, i * D_MODEL:(i + 1) * D_MODEL] for i in range(6)]
    w_gate = w_in_b[:, 6 * D_MODEL:]

    x1, h = _ffn(x.reshape(n, d), ffn1_norm[0], ffn1_w_gu[0], ffn1_w_down[0],
                 next_gain=mix_norm[0])
    h3 = h.reshape(b, s, d)

    q, k = _qk(h, cols[2], cols[3], diff_q_norm[0], diff_k_norm[0], s)
    vt, mq = _vmq(h3, cols[4].T, cols[5], mem_q_norm[0])
    score_bound = (1.01 * LOG2E * DIFF_HEAD_DIM ** 0.5
                   * jnp.max(jnp.abs(diff_q_norm[0])) * jnp.max(jnp.abs(diff_k_norm[0])))
    od = _diff_attention(q.reshape(b, s, d), k.reshape(b, s, d), vt,
                         diff_lambda[0], diff_subln[0], score_bound)

    w_kv = w_mem_kv[0].astype(BF16)
    mk, mv = _memkv(mem.reshape(b * m, d), mem_norm[0], w_kv[:, :D_MODEL],
                    w_kv[:, D_MODEL:], mem_k_norm[0])
    om = _memattn(mq, mk.reshape(b, m, d), mv.reshape(b, m, d))

    x2 = _mix(x1, h, od.reshape(n, d), om.reshape(n, d), s, cols[0], cols[1], conv_w[0],
              conv_b[0], conv_ln_g[0], conv_ln_b[0], w_conv_out[0].astype(BF16),
              w_diff_out[0].astype(BF16), w_mem_out[0].astype(BF16), w_gate, b_gate[0],
              w_o[0].astype(BF16))

    x3 = _ffn(x2, ffn2_norm[0], ffn2_w_gu[0], ffn2_w_down[0])
    return x3.reshape(b, s, d)
```

```python
import functools
import math

import jax
import jax.numpy as jnp
from jax import lax
from jax.experimental import pallas as pl
from jax.experimental.pallas import tpu as pltpu

F32 = jnp.float32
BF16 = jnp.bfloat16

D_MODEL = 1024
D_FF = 2816
CONV_WIDTH = 31
CONV_PAD = CONV_WIDTH // 2
DIFF_HEADS = 8
DIFF_HEAD_DIM = 64
MEM_HEADS = 4
MEM_HEAD_DIM = 256
ROPE_THETA = 10000.0
EPS = 1e-6
LAM_INIT = 0.8 - 0.6 * math.exp(-0.3 * 0)
LOG2E = math.log2(math.e)
Q_SCALE = DIFF_HEAD_DIM ** -0.5 * LOG2E

LANES = 128
SUBLANES = 8
BF16_SUBLANES = 16
MXU_DIM = 256
FF_CHUNK = MXU_DIM
N_FF_CHUNKS = D_FF // FF_CHUNK
VMEM_LIMIT = 56 * 1024 * 1024

TM = 1024
TMX = 512
TQ = 2048
Q_GROUP = MXU_DIM // 2
KEY_CHUNK = 2 * MXU_DIM
TQ_MEM = 2048
CONV_ROWS = 128
HALO = BF16_SUBLANES
PIPE_DEPTH = 1
MAX_SHIFT_BOUND = 60.0


def _params(n_axes):
    return pltpu.CompilerParams(
        dimension_semantics=("arbitrary",) * n_axes,
        vmem_limit_bytes=VMEM_LIMIT)


def _resident(shape):
    zeros = (0,) * len(shape)
    return pl.BlockSpec(shape, lambda *_: zeros, pipeline_mode=pl.Buffered(1))


def _rms(x, gain):
    return x * lax.rsqrt(jnp.mean(x * x, axis=-1, keepdims=True) + EPS) * gain


def _mm(a, b):
    return jnp.dot(a, b, preferred_element_type=F32)


def _mm_nt(a, b):
    return lax.dot_general(a, b, (((1,), (1,)), ((), ())),
                           preferred_element_type=F32)


def _ffn_body(emit_h, x_ref, g_ref, wg_ref, wu_ref, wd_ref, *rest):
    if emit_h:
        g2_ref, o_ref, h_ref, a_ref = rest
    else:
        o_ref, a_ref = rest
    x = x_ref[...]
    xn = _rms(x, g_ref[...]).astype(BF16)
    for c in range(N_FF_CHUNKS):
        cols = slice(c * FF_CHUNK, (c + 1) * FF_CHUNK)
        gate = _mm(xn, wg_ref[:, cols])
        up = _mm(xn, wu_ref[:, cols])
        a_ref[:, cols] = (gate * jax.nn.sigmoid(gate) * up).astype(BF16)
    x1 = x + 0.5 * _mm(a_ref[...], wd_ref[...])
    o_ref[...] = x1
    if emit_h:
        h_ref[...] = _rms(x1, g2_ref[...]).astype(BF16)


def _ffn(x, gain, w_gu, w_down, next_gain=None):
    n = x.shape[0]
    emit_h = next_gain is not None
    wg = w_gu[:, :D_FF].astype(BF16)
    wu = w_gu[:, D_FF:].astype(BF16)
    wd = w_down.astype(BF16)
    row = pl.BlockSpec((TM, D_MODEL), lambda i: (i, 0))
    in_specs = [row, _resident((1, D_MODEL)), _resident(wg.shape),
                _resident(wu.shape), _resident(wd.shape)]
    args = [x, gain.reshape(1, D_MODEL), wg, wu, wd]
    out_shape = [jax.ShapeDtypeStruct((n, D_MODEL), F32)]
    out_specs = [row]
    if emit_h:
        in_specs.append(_resident((1, D_MODEL)))
        args.append(next_gain.reshape(1, D_MODEL))
        out_shape.append(jax.ShapeDtypeStruct((n, D_MODEL), BF16))
        out_specs.append(row)
    out = pl.pallas_call(
        functools.partial(_ffn_body, emit_h),
        grid=(n // TM,),
        in_specs=in_specs, out_specs=out_specs, out_shape=out_shape,
        scratch_shapes=[pltpu.VMEM((TM, D_FF), BF16)],
        compiler_params=_params(1), name="ffn")(*args)
    return out if emit_h else out[0]


def _head_norm_rope(h, w_ref, o_ref, group_mean, direct, crossed):
    lane = lax.broadcasted_iota(jnp.int32, (h.shape[0], LANES), 1)
    upper_half = (lane & (DIFF_HEAD_DIM // 2)) != 0
    width = group_mean.shape[0]
    y = _mm(h, w_ref[...])
    for g in range(w_ref.shape[1] // width):
        yg = y[:, g * width:(g + 1) * width]
        ms = _mm((yg * yg).astype(BF16), group_mean)
        yn = yg * lax.rsqrt(ms + EPS)
        for j in range(width // LANES):
            t = yn[:, j * LANES:(j + 1) * LANES]
            swapped = jnp.where(upper_half,
                                pltpu.roll(t, DIFF_HEAD_DIM // 2, 1),
                                pltpu.roll(t, LANES - DIFF_HEAD_DIM // 2, 1))
            lanes = slice(g * width + j * LANES, g * width + (j + 1) * LANES)
            o_ref[:, lanes] = (t * direct + swapped * crossed).astype(BF16)


def _qk_body(h_ref, wq_ref, wk_ref, gm_ref, qd_ref, qc_ref, kd_ref, kc_ref,
             q_ref, k_ref):
    h = h_ref[...]
    gm = gm_ref[...]
    _head_norm_rope(h, wq_ref, q_ref, gm, qd_ref[...], qc_ref[...])
    _head_norm_rope(h, wk_ref, k_ref, gm, kd_ref[...], kc_ref[...])


def _rope_tables(gain, scale, cos, sin):
    half = DIFF_HEAD_DIM // 2
    reps = LANES // DIFF_HEAD_DIM
    swapped_gain = jnp.concatenate([gain[half:], gain[:half]])
    direct = jnp.tile(jnp.concatenate([cos, cos], axis=1) * (gain * scale), (1, reps))
    crossed = jnp.tile(jnp.concatenate([-sin, sin], axis=1) * (swapped_gain * scale),
                       (1, reps))
    return direct, crossed


def _qk(h, wq, wk, gq, gk, seq):
    n = h.shape[0]
    inv_freq = 1.0 / (ROPE_THETA ** (jnp.arange(0, DIFF_HEAD_DIM, 2, dtype=F32)
                                     / DIFF_HEAD_DIM))
    ang = jnp.arange(seq, dtype=F32)[:, None] * inv_freq[None, :]
    cos, sin = jnp.cos(ang), jnp.sin(ang)
    qd, qc = _rope_tables(gq, Q_SCALE, cos, sin)
    kd, kc = _rope_tables(gk, 1.0, cos, sin)
    ids = jnp.arange(MXU_DIM) // DIFF_HEAD_DIM
    group_mean = jnp.where(ids[:, None] == ids[None, :], 1.0 / DIFF_HEAD_DIM,
                           0.0).astype(BF16)
    row = pl.BlockSpec((TM, D_MODEL), lambda i: (i, 0))
    tab = pl.BlockSpec((TM, LANES), lambda i: (i % (seq // TM), 0))
    out = jax.ShapeDtypeStruct((n, D_MODEL), BF16)
    return pl.pallas_call(
        _qk_body, grid=(n // TM,),
        in_specs=[row, _resident(wq.shape), _resident(wk.shape),
                  _resident(group_mean.shape), tab, tab, tab, tab],
        out_specs=[row, row], out_shape=[out, out],
        compiler_params=_params(1), name="qk")(
            h, wq, wk, group_mean, qd, qc, kd, kc)


def _vmq_body(h_ref, wvt_ref, wm_ref, gm_ref, vt_ref, mq_ref):
    h = h_ref[0]
    vt_ref[0] = _mm_nt(wvt_ref[...], h).astype(BF16)
    y = _mm(h, wm_ref[...])
    scale = MEM_HEAD_DIM ** -0.5
    for g in range(MEM_HEADS):
        sl = slice(g * MEM_HEAD_DIM, (g + 1) * MEM_HEAD_DIM)
        mq_ref[0, :, sl] = (_rms(y[:, sl], gm_ref[...]) * scale).astype(BF16)


def _vmq(h, wvt, wm, gm):
    b, s, d = h.shape
    row = pl.BlockSpec((1, TM, d), lambda bi, i: (bi, i, 0))
    col = pl.BlockSpec((1, d, TM), lambda bi, i: (bi, 0, i))
    return pl.pallas_call(
        _vmq_body, grid=(b, s // TM),
        in_specs=[row, _resident(wvt.shape), _resident(wm.shape),
                  _resident((1, MEM_HEAD_DIM))],
        out_specs=[col, row],
        out_shape=[jax.ShapeDtypeStruct((b, d, s), BF16),
                   jax.ShapeDtypeStruct((b, s, d), BF16)],
        compiler_params=_params(2), name="vmq")(
            h, wvt, wm, gm.reshape(1, MEM_HEAD_DIM))


def _diff_body(bound_ref, lam_ref, sg_ref, q_ref, k_ref, vt_ref, o_ref):
    lp = lam_ref[...]
    lam = (jnp.exp(jnp.sum(lp[0:1] * lp[1:2], axis=-1, keepdims=True))
           - jnp.exp(jnp.sum(lp[2:3] * lp[3:4], axis=-1, keepdims=True)) + LAM_INIT)
    bound = bound_ref[0]
    n_keys = k_ref.shape[1]
    lane = lax.broadcasted_iota(jnp.int32, (Q_GROUP, LANES), 1)
    first = lane < DIFF_HEAD_DIM

    def shifted_by_bound(qq):
        ot = jnp.zeros((LANES, MXU_DIM), F32)
        part = jnp.zeros((SUBLANES, MXU_DIM), F32)
        for c in range(n_keys // KEY_CHUNK):
            keys = slice(c * KEY_CHUNK, (c + 1) * KEY_CHUNK)
            et = jnp.exp2(_mm_nt(k_ref[0, keys, :], qq) - bound)
            part = part + jnp.sum(
                et.reshape(KEY_CHUNK // SUBLANES, SUBLANES, MXU_DIM), axis=0)
            ot = ot + _mm(vt_ref[0, :, keys], et.astype(BF16))
        return ot / jnp.sum(part, axis=0, keepdims=True)

    def shifted_by_max(qq):
        st = _mm_nt(k_ref[0], qq)
        et = jnp.exp2(st - jnp.max(st, axis=0, keepdims=True))
        return _mm(vt_ref[0], et.astype(BF16)) / jnp.sum(et, axis=0, keepdims=True)

    def attend(softmax_pv):
        for g in range(TQ // Q_GROUP):
            rows = slice(g * Q_GROUP, (g + 1) * Q_GROUP)
            q = q_ref[0, rows, :]
            zero = jnp.zeros_like(q)
            qq = jnp.concatenate([jnp.where(first, q, zero), jnp.where(first, zero, q)],
                                 axis=0)
            ot = softmax_pv(qq)
            odt = ot[:, :Q_GROUP] - lam * ot[:, Q_GROUP:]
            ms = jnp.mean(odt * odt, axis=0, keepdims=True)
            y = odt * lax.rsqrt(ms + EPS) * sg_ref[...] * (1.0 - LAM_INIT)
            o_ref[0, rows, :] = y.T.astype(BF16)

    use_bound = bound <= MAX_SHIFT_BOUND

    @pl.when(use_bound)
    def _():
        attend(shifted_by_bound)

    @pl.when(jnp.logical_not(use_bound))
    def _():
        attend(shifted_by_max)


def _diff_attention(q, k, vt, diff_lambda, subln, score_bound):
    b, s, _ = q.shape
    width = 2 * DIFF_HEAD_DIM
    qblk = pl.BlockSpec((1, TQ, width), lambda bi, hi, i: (bi, i, hi))
    kblk = pl.BlockSpec((1, s, width), lambda bi, hi, i: (bi, 0, hi))
    vblk = pl.BlockSpec((1, width, s), lambda bi, hi, i: (bi, hi, 0))
    gain = jnp.broadcast_to(subln.reshape(width, 1), (width, Q_GROUP))
    return pl.pallas_call(
        _diff_body, grid=(b, DIFF_HEADS, s // TQ),
        in_specs=[pl.BlockSpec(memory_space=pltpu.SMEM),
                  _resident((4, DIFF_HEAD_DIM)), _resident((width, Q_GROUP)),
                  qblk, kblk, vblk],
        out_specs=qblk, out_shape=jax.ShapeDtypeStruct(q.shape, BF16),
        compiler_params=_params(3), name="diffattn")(
            score_bound.reshape(1), diff_lambda, gain, q, k, vt)


def _memkv_body(m_ref, g_ref, wk_ref, wv_ref, gk_ref, k_ref, v_ref):
    xn = _rms(m_ref[...], g_ref[...]).astype(BF16)
    v_ref[...] = _mm(xn, wv_ref[...]).astype(BF16)
    y = _mm(xn, wk_ref[...])
    for g in range(MEM_HEADS):
        sl = slice(g * MEM_HEAD_DIM, (g + 1) * MEM_HEAD_DIM)
        k_ref[:, sl] = _rms(y[:, sl], gk_ref[...]).astype(BF16)


def _memkv(mem, gain, wk, wv, gk):
    n = mem.shape[0]
    row = pl.BlockSpec((TM, D_MODEL), lambda i: (i, 0))
    out = jax.ShapeDtypeStruct((n, D_MODEL), BF16)
    return pl.pallas_call(
        _memkv_body, grid=(n // TM,),
        in_specs=[row, _resident((1, D_MODEL)), _resident(wk.shape),
                  _resident(wv.shape), _resident((1, MEM_HEAD_DIM))],
        out_specs=[row, row], out_shape=[out, out],
        compiler_params=_params(1), name="memkv")(
            mem, gain.reshape(1, D_MODEL), wk, wv, gk.reshape(1, MEM_HEAD_DIM))


def _memattn_body(q_ref, k_ref, v_ref, o_ref):
    for g in range(MEM_HEADS):
        sl = slice(g * MEM_HEAD_DIM, (g + 1) * MEM_HEAD_DIM)
        s = _mm_nt(q_ref[0, :, sl], k_ref[0, :, sl])
        e = jnp.exp(s - jnp.max(s, axis=-1, keepdims=True))
        l = jnp.sum(e, axis=-1, keepdims=True)
        o_ref[0, :, sl] = (_mm(e.astype(BF16), v_ref[0, :, sl]) / l).astype(BF16)


def _memattn(q, k, v):
    b, s, w = q.shape
    m = k.shape[1]
    qblk = pl.BlockSpec((1, TQ_MEM, w), lambda bi, i: (bi, i, 0))
    kvblk = pl.BlockSpec((1, m, w), lambda bi, i: (bi, 0, 0))
    return pl.pallas_call(
        _memattn_body, grid=(b, s // TQ_MEM),
        in_specs=[qblk, kvblk, kvblk], out_specs=qblk,
        out_shape=jax.ShapeDtypeStruct(q.shape, BF16),
        compiler_params=_params(2), name="memattn")(q, k, v)


def _mix_body(tiles_per_seq, hp_ref, hc_ref, hn_ref, x_ref, h_ref, d_ref, m_ref, wa_ref,
              wb_ref, cw_ref, cb_ref, lg_ref, lb_ref, wc_ref, wd_ref, wm_ref, wg_ref,
              bg_ref, wo_ref, o_ref, xp_ref, xs_ref, y_ref):
    t = pl.program_id(0)
    rows = TMX + 2 * HALO

    @pl.when(t == 0)
    def _():
        y_ref[...] = jnp.zeros_like(y_ref)

    y = y_ref[...] + cb_ref[...]

    pos = jnp.minimum(t, pl.num_programs(0) - 1 - PIPE_DEPTH) % tiles_per_seq
    hcat = jnp.concatenate([hp_ref[...], hc_ref[...], hn_ref[...]], axis=0)
    u = _mm(hcat, wa_ref[...]) * jax.nn.sigmoid(_mm(hcat, wb_ref[...]))
    r = lax.broadcasted_iota(jnp.int32, (rows, 1), 0)
    lo = jnp.where(pos > 0, 0, HALO)
    hi = jnp.where(pos < tiles_per_seq - 1, rows, HALO + TMX)
    xp_ref[...] = jnp.where((r >= lo) & (r < hi), u, 0.0)

    base = HALO - CONV_PAD
    span = rows - SUBLANES

    for j in range(D_MODEL // LANES):
        lanes = slice(j * LANES, (j + 1) * LANES)
        for p in range(1, SUBLANES):
            xs_ref[p - 1, 0:span, :] = xp_ref[p:p + span, lanes]
        for r0 in range(0, TMX, CONV_ROWS):
            acc = jnp.zeros((CONV_ROWS, LANES), F32)
            for k in range(CONV_WIDTH):
                p = (base + k) % SUBLANES
                start = base + k - p + r0
                if p == 0:
                    win = xp_ref[start:start + CONV_ROWS, lanes]
                else:
                    win = xs_ref[p - 1, start:start + CONV_ROWS, :]
                acc = acc + win * cw_ref[k:k + 1, lanes]
            y_ref[r0:r0 + CONV_ROWS, lanes] = acc

    mu = jnp.mean(y, axis=-1, keepdims=True)
    yc = y - mu
    var = jnp.mean(yc * yc, axis=-1, keepdims=True)
    z = yc * lax.rsqrt(var + EPS) * lg_ref[...] + lb_ref[...]
    c = (z * jax.nn.sigmoid(z)).astype(BF16)
    gates = jax.nn.sigmoid(_mm(h_ref[...], wg_ref[...]) + bg_ref[...])
    merged = (gates[:, :D_MODEL] * _mm(c, wc_ref[...])
              + gates[:, D_MODEL:2 * D_MODEL] * _mm(d_ref[...], wd_ref[...])
              + gates[:, 2 * D_MODEL:] * _mm(m_ref[...], wm_ref[...]))
    o_ref[...] = x_ref[...] + _mm(merged.astype(BF16), wo_ref[...])


def _mix(x1, h, od, om, seq, wa, wb, conv_w, conv_b, ln_g, ln_b, wc, wd, wm, wg, bg, wo):
    n, d = x1.shape
    tiles = n // TMX
    per = TMX // HALO
    last_halo = n // HALO - 1

    def conv_tile(t):
        return jnp.minimum(t, tiles - 1)

    conv_row = pl.BlockSpec((TMX, d), lambda t: (conv_tile(t), 0))
    prev = pl.BlockSpec((HALO, d), lambda t: (jnp.maximum(conv_tile(t) * per - 1, 0), 0))
    nxt = pl.BlockSpec((HALO, d),
                       lambda t: (jnp.minimum((conv_tile(t) + 1) * per, last_halo), 0))
    row = pl.BlockSpec((TMX, d), lambda t: (jnp.maximum(t - PIPE_DEPTH, 0), 0))
    vec = _resident((1, d))
    return pl.pallas_call(
        functools.partial(_mix_body, seq // TMX), grid=(tiles + PIPE_DEPTH,),
        in_specs=[prev, conv_row, nxt, row, row, row, row, _resident(wa.shape),
                  _resident(wb.shape), _resident((CONV_WIDTH, d)), vec, vec, vec,
                  _resident(wc.shape), _resident(wd.shape), _resident(wm.shape),
                  _resident(wg.shape), _resident((1, 3 * d)), _resident(wo.shape)],
        out_specs=row, out_shape=jax.ShapeDtypeStruct((n, d), F32),
        scratch_shapes=[pltpu.VMEM((TMX + 2 * HALO, d), F32),
                        pltpu.VMEM((SUBLANES - 1, TMX + 2 * HALO, LANES), F32),
                        pltpu.VMEM((TMX, d), F32)],
        compiler_params=_params(1), name="mix")(
            h, h, h, x1, h, od, om, wa, wb, conv_w.reshape(CONV_WIDTH, d),
            conv_b.reshape(1, d), ln_g.reshape(1, d), ln_b.reshape(1, d),
            wc, wd, wm, wg, bg.reshape(1, 3 * d), wo)


def kernel(x, mem, ffn1_norm, ffn1_w_gu, ffn1_w_down, mix_norm, mem_norm, w_in, b_gate,
           conv_w, conv_b, conv_ln_g, conv_ln_b, w_conv_out, diff_q_norm, diff_k_norm,
           diff_lambda, diff_subln, w_diff_out, w_mem_kv, mem_q_norm, mem_k_norm,
           w_mem_out, w_o, ffn2_norm, ffn2_w_gu, ffn2_w_down):
    b, s, d = x.shape
    m = mem.shape[1]
    n = b * s
    assert d == D_MODEL and ffn1_norm.shape[0] == 1
    assert s % TM == 0 and s % TQ == 0 and s % TQ_MEM == 0 and s % KEY_CHUNK == 0
    assert (b * m) % TM == 0 and TMX % CONV_ROWS == 0 and HALO >= CONV_PAD
    assert s % TMX == 0

    w_in_b = w_in[0].astype(BF16)
    cols = [w_in_b[:, i * D_MODEL:(i + 1) * D_MODEL] for i in range(6)]
    w_gate = w_in_b[:, 6 * D_MODEL:]

    x1, h = _ffn(x.reshape(n, d), ffn1_norm[0], ffn1_w_gu[0], ffn1_w_down[0],
                 next_gain=mix_norm[0])
    h3 = h.reshape(b, s, d)

    q, k = _qk(h, cols[2], cols[3], diff_q_norm[0], diff_k_norm[0], s)
    vt, mq = _vmq(h3, cols[4].T, cols[5], mem_q_norm[0])
    score_bound = (1.01 * LOG2E * DIFF_HEAD_DIM ** 0.5
                   * jnp.max(jnp.abs(diff_q_norm[0])) * jnp.max(jnp.abs(diff_k_norm[0])))
    od = _diff_attention(q.reshape(b, s, d), k.reshape(b, s, d), vt,
                         diff_lambda[0], diff_subln[0], score_bound)

    w_kv = w_mem_kv[0].astype(BF16)
    mk, mv = _memkv(mem.reshape(b * m, d), mem_norm[0], w_kv[:, :D_MODEL],
                    w_kv[:, D_MODEL:], mem_k_norm[0])
    om = _memattn(mq, mk.reshape(b, m, d), mv.reshape(b, m, d))

    x2 = _mix(x1, h, od.reshape(n, d), om.reshape(n, d), s, cols[0], cols[1], conv_w[0],
              conv_b[0], conv_ln_g[0], conv_ln_b[0], w_conv_out[0].astype(BF16),
              w_diff_out[0].astype(BF16), w_mem_out[0].astype(BF16), w_gate, b_gate[0],
              w_o[0].astype(BF16))

    x3 = _ffn(x2, ffn2_norm[0], ffn2_w_gu[0], ffn2_w_down[0])
    return x3.reshape(b, s, d)
```

```python
import functools
import math

import jax
import jax.numpy as jnp
from jax import lax
from jax.experimental import pallas as pl
from jax.experimental.pallas import tpu as pltpu

F32 = jnp.float32
BF16 = jnp.bfloat16

D_MODEL = 1024
D_FF = 2816
CONV_WIDTH = 31
CONV_PAD = CONV_WIDTH // 2
DIFF_HEADS = 8
DIFF_HEAD_DIM = 64
MEM_HEADS = 4
MEM_HEAD_DIM = 256
ROPE_THETA = 10000.0
EPS = 1e-6
LAM_INIT = 0.8 - 0.6 * math.exp(-0.3 * 0)
LOG2E = math.log2(math.e)
Q_SCALE = DIFF_HEAD_DIM ** -0.5 * LOG2E

LANES = 128
SUBLANES = 8
BF16_SUBLANES = 16
MXU_DIM = 256
FF_CHUNK = MXU_DIM
N_FF_CHUNKS = D_FF // FF_CHUNK
VMEM_LIMIT = 56 * 1024 * 1024

TM = 1024
TMX = 512
TQ = 2048
Q_GROUP = MXU_DIM // 2
KEY_CHUNK = 2 * MXU_DIM
TQ_MEM = 2048
CONV_ROWS = 128
HALO = BF16_SUBLANES
PIPE_DEPTH = 1
MAX_SHIFT_BOUND = 60.0


def _params(n_axes):
    return pltpu.CompilerParams(
        dimension_semantics=("arbitrary",) * n_axes,
        vmem_limit_bytes=VMEM_LIMIT)


def _resident(shape):
    zeros = (0,) * len(shape)
    return pl.BlockSpec(shape, lambda *_: zeros, pipeline_mode=pl.Buffered(1))


def _rms(x, gain):
    return x * lax.rsqrt(jnp.mean(x * x, axis=-1, keepdims=True) + EPS) * gain


def _mm(a, b):
    return jnp.dot(a, b, preferred_element_type=F32)


def _mm_nt(a, b):
    return lax.dot_general(a, b, (((1,), (1,)), ((), ())),
                           preferred_element_type=F32)


def _ffn_body(emit_h, x_ref, g_ref, wg_ref, wu_ref, wd_ref, *rest):
    if emit_h:
        g2_ref, o_ref, h_ref, a_ref = rest
    else:
        o_ref, a_ref = rest
    x = x_ref[...]
    xn = _rms(x, g_ref[...]).astype(BF16)
    for c in range(N_FF_CHUNKS):
        cols = slice(c * FF_CHUNK, (c + 1) * FF_CHUNK)
        gate = _mm(xn, wg_ref[:, cols])
        up = _mm(xn, wu_ref[:, cols])
        a_ref[:, cols] = (gate * jax.nn.sigmoid(gate) * up).astype(BF16)
    x1 = x + 0.5 * _mm(a_ref[...], wd_ref[...])
    o_ref[...] = x1
    if emit_h:
        h_ref[...] = _rms(x1, g2_ref[...]).astype(BF16)


def _ffn(x, gain, w_gu, w_down, next_gain=None):
    n = x.shape[0]
    emit_h = next_gain is not None
    wg = w_gu[:, :D_FF].astype(BF16)
    wu = w_gu[:, D_FF:].astype(BF16)
    wd = w_down.astype(BF16)
    row = pl.BlockSpec((TM, D_MODEL), lambda i: (i, 0))
    in_specs = [row, _resident((1, D_MODEL)), _resident(wg.shape),
                _resident(wu.shape), _resident(wd.shape)]
    args = [x, gain.reshape(1, D_MODEL), wg, wu, wd]
    out_shape = [jax.ShapeDtypeStruct((n, D_MODEL), F32)]
    out_specs = [row]
    if emit_h:
        in_specs.append(_resident((1, D_MODEL)))
        args.append(next_gain.reshape(1, D_MODEL))
        out_shape.append(jax.ShapeDtypeStruct((n, D_MODEL), BF16))
        out_specs.append(row)
    out = pl.pallas_call(
        functools.partial(_ffn_body, emit_h),
        grid=(n // TM,),
        in_specs=in_specs, out_specs=out_specs, out_shape=out_shape,
        scratch_shapes=[pltpu.VMEM((TM, D_FF), BF16)],
        compiler_params=_params(1), name="ffn")(*args)
    return out if emit_h else out[0]


def _head_norm_rope(h, w_ref, o_ref, group_mean, direct, crossed):
    lane = lax.broadcasted_iota(jnp.int32, (h.shape[0], LANES), 1)
    upper_half = (lane & (DIFF_HEAD_DIM // 2)) != 0
    width = group_mean.shape[0]
    y = _mm(h, w_ref[...])
    for g in range(w_ref.shape[1] // width):
        yg = y[:, g * width:(g + 1) * width]
        ms = _mm((yg * yg).astype(BF16), group_mean)
        yn = yg * lax.rsqrt(ms + EPS)
        for j in range(width // LANES):
            t = yn[:, j * LANES:(j + 1) * LANES]
            swapped = jnp.where(upper_half,
                                pltpu.roll(t, DIFF_HEAD_DIM // 2, 1),
                                pltpu.roll(t, LANES - DIFF_HEAD_DIM // 2, 1))
            lanes = slice(g * width + j * LANES, g * width + (j + 1) * LANES)
            o_ref[:, lanes] = (t * direct + swapped * crossed).astype(BF16)


def _qk_body(h_ref, wq_ref, wk_ref, gm_ref, qd_ref, qc_ref, kd_ref, kc_ref,
             q_ref, k_ref):
    h = h_ref[...]
    gm = gm_ref[...]
    _head_norm_rope(h, wq_ref, q_ref, gm, qd_ref[...], qc_ref[...])
    _head_norm_rope(h, wk_ref, k_ref, gm, kd_ref[...], kc_ref[...])


def _rope_tables(gain, scale, cos, sin):
    half = DIFF_HEAD_DIM // 2
    reps = LANES // DIFF_HEAD_DIM
    swapped_gain = jnp.concatenate([gain[half:], gain[:half]])
    direct = jnp.tile(jnp.concatenate([cos, cos], axis=1) * (gain * scale), (1, reps))
    crossed = jnp.tile(jnp.concatenate([-sin, sin], axis=1) * (swapped_gain * scale),
                       (1, reps))
    return direct, crossed


def _qk(h, wq, wk, gq, gk, seq):
    n = h.shape[0]
    inv_freq = 1.0 / (ROPE_THETA ** (jnp.arange(0, DIFF_HEAD_DIM, 2, dtype=F32)
                                     / DIFF_HEAD_DIM))
    ang = jnp.arange(seq, dtype=F32)[:, None] * inv_freq[None, :]
    cos, sin = jnp.cos(ang), jnp.sin(ang)
    qd, qc = _rope_tables(gq, Q_SCALE, cos, sin)
    kd, kc = _rope_tables(gk, 1.0, cos, sin)
    ids = jnp.arange(MXU_DIM) // DIFF_HEAD_DIM
    group_mean = jnp.where(ids[:, None] == ids[None, :], 1.0 / DIFF_HEAD_DIM,
                           0.0).astype(BF16)
    row = pl.BlockSpec((TM, D_MODEL), lambda i: (i, 0))
    tab = pl.BlockSpec((TM, LANES), lambda i: (i % (seq // TM), 0))
    out = jax.ShapeDtypeStruct((n, D_MODEL), BF16)
    return pl.pallas_call(
        _qk_body, grid=(n // TM,),
        in_specs=[row, _resident(wq.shape), _resident(wk.shape),
                  _resident(group_mean.shape), tab, tab, tab, tab],
        out_specs=[row, row], out_shape=[out, out],
        compiler_params=_params(1), name="qk")(
            h, wq, wk, group_mean, qd, qc, kd, kc)


def _vmq_body(h_ref, wvt_ref, wm_ref, gm_ref, vt_ref, mq_ref):
    h = h_ref[0]
    vt_ref[0] = _mm_nt(wvt_ref[...], h).astype(BF16)
    y = _mm(h, wm_ref[...])
    scale = MEM_HEAD_DIM ** -0.5
    for g in range(MEM_HEADS):
        sl = slice(g * MEM_HEAD_DIM, (g + 1) * MEM_HEAD_DIM)
        mq_ref[0, :, sl] = (_rms(y[:, sl], gm_ref[...]) * scale).astype(BF16)


def _vmq(h, wvt, wm, gm):
    b, s, d = h.shape
    row = pl.BlockSpec((1, TM, d), lambda bi, i: (bi, i, 0))
    col = pl.BlockSpec((1, d, TM), lambda bi, i: (bi, 0, i))
    return pl.pallas_call(
        _vmq_body, grid=(b, s // TM),
        in_specs=[row, _resident(wvt.shape), _resident(wm.shape),
                  _resident((1, MEM_HEAD_DIM))],
        out_specs=[col, row],
        out_shape=[jax.ShapeDtypeStruct((b, d, s), BF16),
                   jax.ShapeDtypeStruct((b, s, d), BF16)],
        compiler_params=_params(2), name="vmq")(
            h, wvt, wm, gm.reshape(1, MEM_HEAD_DIM))


def _diff_body(bound_ref, lam_ref, sg_ref, q_ref, k_ref, vt_ref, o_ref):
    lp = lam_ref[...]
    lam = (jnp.exp(jnp.sum(lp[0:1] * lp[1:2], axis=-1, keepdims=True))
           - jnp.exp(jnp.sum(lp[2:3] * lp[3:4], axis=-1, keepdims=True)) + LAM_INIT)
    bound = bound_ref[0]
    n_keys = k_ref.shape[1]
    lane = lax.broadcasted_iota(jnp.int32, (Q_GROUP, LANES), 1)
    first = lane < DIFF_HEAD_DIM

    def shifted_by_bound(qq):
        ot = jnp.zeros((LANES, MXU_DIM), F32)
        part = jnp.zeros((SUBLANES, MXU_DIM), F32)
        for c in range(n_keys // KEY_CHUNK):
            keys = slice(c * KEY_CHUNK, (c + 1) * KEY_CHUNK)
            et = jnp.exp2(_mm_nt(k_ref[0, keys, :], qq) - bound)
            part = part + jnp.sum(
                et.reshape(KEY_CHUNK // SUBLANES, SUBLANES, MXU_DIM), axis=0)
            ot = ot + _mm(vt_ref[0, :, keys], et.astype(BF16))
        return ot / jnp.sum(part, axis=0, keepdims=True)

    def shifted_by_max(qq):
        st = _mm_nt(k_ref[0], qq)
        et = jnp.exp2(st - jnp.max(st, axis=0, keepdims=True))
        return _mm(vt_ref[0], et.astype(BF16)) / jnp.sum(et, axis=0, keepdims=True)

    def attend(softmax_pv):
        for g in range(TQ // Q_GROUP):
            rows = slice(g * Q_GROUP, (g + 1) * Q_GROUP)
            q = q_ref[0, rows, :]
            zero = jnp.zeros_like(q)
            qq = jnp.concatenate([jnp.where(first, q, zero), jnp.where(first, zero, q)],
                                 axis=0)
            ot = softmax_pv(qq)
            odt = ot[:, :Q_GROUP] - lam * ot[:, Q_GROUP:]
            ms = jnp.mean(odt * odt, axis=0, keepdims=True)
            y = odt * lax.rsqrt(ms + EPS) * sg_ref[...] * (1.0 - LAM_INIT)
            o_ref[0, rows, :] = y.T.astype(BF16)

    use_bound = bound <= MAX_SHIFT_BOUND

    @pl.when(use_bound)
    def _():
        attend(shifted_by_bound)

    @pl.when(jnp.logical_not(use_bound))
    def _():
        attend(shifted_by_max)


def _diff_attention(q, k, vt, diff_lambda, subln, score_bound):
    b, s, _ = q.shape
    width = 2 * DIFF_HEAD_DIM
    qblk = pl.BlockSpec((1, TQ, width), lambda bi, hi, i: (bi, i, hi))
    kblk = pl.BlockSpec((1, s, width), lambda bi, hi, i: (bi, 0, hi))
    vblk = pl.BlockSpec((1, width, s), lambda bi, hi, i: (bi, hi, 0))
    gain = jnp.broadcast_to(subln.reshape(width, 1), (width, Q_GROUP))
    return pl.pallas_call(
        _diff_body, grid=(b, DIFF_HEADS, s // TQ),
        in_specs=[pl.BlockSpec(memory_space=pltpu.SMEM),
                  _resident((4, DIFF_HEAD_DIM)), _resident((width, Q_GROUP)),
                  qblk, kblk, vblk],
        out_specs=qblk, out_shape=jax.ShapeDtypeStruct(q.shape, BF16),
        compiler_params=_params(3), name="diffattn")(
            score_bound.reshape(1), diff_lambda, gain, q, k, vt)


def _memkv_body(m_ref, g_ref, wk_ref, wv_ref, gk_ref, k_ref, v_ref):
    xn = _rms(m_ref[...], g_ref[...]).astype(BF16)
    v_ref[...] = _mm(xn, wv_ref[...]).astype(BF16)
    y = _mm(xn, wk_ref[...])
    for g in range(MEM_HEADS):
        sl = slice(g * MEM_HEAD_DIM, (g + 1) * MEM_HEAD_DIM)
        k_ref[:, sl] = _rms(y[:, sl], gk_ref[...]).astype(BF16)


def _memkv(mem, gain, wk, wv, gk):
    n = mem.shape[0]
    row = pl.BlockSpec((TM, D_MODEL), lambda i: (i, 0))
    out = jax.ShapeDtypeStruct((n, D_MODEL), BF16)
    return pl.pallas_call(
        _memkv_body, grid=(n // TM,),
        in_specs=[row, _resident((1, D_MODEL)), _resident(wk.shape),
                  _resident(wv.shape), _resident((1, MEM_HEAD_DIM))],
        out_specs=[row, row], out_shape=[out, out],
        compiler_params=_params(1), name="memkv")(
            mem, gain.reshape(1, D_MODEL), wk, wv, gk.reshape(1, MEM_HEAD_DIM))


def _memattn_body(q_ref, k_ref, v_ref, o_ref):
    for g in range(MEM_HEADS):
        sl = slice(g * MEM_HEAD_DIM, (g + 1) * MEM_HEAD_DIM)
        s = _mm_nt(q_ref[0, :, sl], k_ref[0, :, sl])
        e = jnp.exp(s - jnp.max(s, axis=-1, keepdims=True))
        l = jnp.sum(e, axis=-1, keepdims=True)
        o_ref[0, :, sl] = (_mm(e.astype(BF16), v_ref[0, :, sl]) / l).astype(BF16)


def _memattn(q, k, v):
    b, s, w = q.shape
    m = k.shape[1]
    qblk = pl.BlockSpec((1, TQ_MEM, w), lambda bi, i: (bi, i, 0))
    kvblk = pl.BlockSpec((1, m, w), lambda bi, i: (bi, 0, 0))
    return pl.pallas_call(
        _memattn_body, grid=(b, s // TQ_MEM),
        in_specs=[qblk, kvblk, kvblk], out_specs=qblk,
        out_shape=jax.ShapeDtypeStruct(q.shape, BF16),
        compiler_params=_params(2), name="memattn")(q, k, v)


def _mix_body(tiles_per_seq, hp_ref, hc_ref, hn_ref, x_ref, h_ref, d_ref, m_ref, wa_ref,
              wb_ref, cw_ref, cb_ref, lg_ref, lb_ref, wc_ref, wd_ref, wm_ref, wg_ref,
              bg_ref, wo_ref, o_ref, xp_ref, xs_ref, y_ref):
    t = pl.program_id(0)
    rows = TMX + 2 * HALO

    @pl.when(t == 0)
    def _():
        y_ref[...] = jnp.zeros_like(y_ref)

    y = y_ref[...] + cb_ref[...]

    pos = jnp.minimum(t, pl.num_programs(0) - 1 - PIPE_DEPTH) % tiles_per_seq
    hcat = jnp.concatenate([hp_ref[...], hc_ref[...], hn_ref[...]], axis=0)
    u = _mm(hcat, wa_ref[...]) * jax.nn.sigmoid(_mm(hcat, wb_ref[...]))
    r = lax.broadcasted_iota(jnp.int32, (rows, 1), 0)
    lo = jnp.where(pos > 0, 0, HALO)
    hi = jnp.where(pos < tiles_per_seq - 1, rows, HALO + TMX)
    xp_ref[...] = jnp.where((r >= lo) & (r < hi), u, 0.0)

    base = HALO - CONV_PAD
    span = rows - SUBLANES

    for j in range(D_MODEL // LANES):
        lanes = slice(j * LANES, (j + 1) * LANES)
        xs_ref[0] = xp_ref[:, lanes]
        for p in range(1, SUBLANES):
            xs_ref[p, 0:span, :] = xs_ref[0, p:p + span, :]
        for r0 in range(0, TMX, CONV_ROWS):
            acc = jnp.zeros((CONV_ROWS, LANES), F32)
            for k in range(CONV_WIDTH):
                p = (base + k) % SUBLANES
                start = base + k - p + r0
                acc = acc + (xs_ref[p, start:start + CONV_ROWS, :]
                             * cw_ref[k:k + 1, lanes])
            y_ref[r0:r0 + CONV_ROWS, lanes] = acc

    mu = jnp.mean(y, axis=-1, keepdims=True)
    yc = y - mu
    var = jnp.mean(yc * yc, axis=-1, keepdims=True)
    z = yc * lax.rsqrt(var + EPS) * lg_ref[...] + lb_ref[...]
    c = (z * jax.nn.sigmoid(z)).astype(BF16)
    gates = jax.nn.sigmoid(_mm(h_ref[...], wg_ref[...]) + bg_ref[...])
    merged = (gates[:, :D_MODEL] * _mm(c, wc_ref[...])
              + gates[:, D_MODEL:2 * D_MODEL] * _mm(d_ref[...], wd_ref[...])
              + gates[:, 2 * D_MODEL:] * _mm(m_ref[...], wm_ref[...]))
    o_ref[...] = x_ref[...] + _mm(merged.astype(BF16), wo_ref[...])


def _mix(x1, h, od, om, seq, wa, wb, conv_w, conv_b, ln_g, ln_b, wc, wd, wm, wg, bg, wo):
    n, d = x1.shape
    tiles = n // TMX
    per = TMX // HALO
    last_halo = n // HALO - 1

    def conv_tile(t):
        return jnp.minimum(t, tiles - 1)

    conv_row = pl.BlockSpec((TMX, d), lambda t: (conv_tile(t), 0))
    prev = pl.BlockSpec((HALO, d), lambda t: (jnp.maximum(conv_tile(t) * per - 1, 0), 0))
    nxt = pl.BlockSpec((HALO, d),
                       lambda t: (jnp.minimum((conv_tile(t) + 1) * per, last_halo), 0))
    row = pl.BlockSpec((TMX, d), lambda t: (jnp.maximum(t - PIPE_DEPTH, 0), 0))
    vec = _resident((1, d))
    return pl.pallas_call(
        functools.partial(_mix_body, seq // TMX), grid=(tiles + PIPE_DEPTH,),
        in_specs=[prev, conv_row, nxt, row, row, row, row, _resident(wa.shape),
                  _resident(wb.shape), _resident((CONV_WIDTH, d)), vec, vec, vec,
                  _resident(wc.shape), _resident(wd.shape), _resident(wm.shape),
                  _resident(wg.shape), _resident((1, 3 * d)), _resident(wo.shape)],
        out_specs=row, out_shape=jax.ShapeDtypeStruct((n, d), F32),
        scratch_shapes=[pltpu.VMEM((TMX + 2 * HALO, d), F32),
                        pltpu.VMEM((SUBLANES, TMX + 2 * HALO, LANES), F32),
                        pltpu.VMEM((TMX, d), F32)],
        compiler_params=_params(1), name="mix")(
            h, h, h, x1, h, od, om, wa, wb, conv_w.reshape(CONV_WIDTH, d),
            conv_b.reshape(1, d), ln_g.reshape(1, d), ln_b.reshape(1, d),
            wc, wd, wm, wg, bg.reshape(1, 3 * d), wo)


def kernel(x, mem, ffn1_norm, ffn1_w_gu, ffn1_w_down, mix_norm, mem_norm, w_in, b_gate,
           conv_w, conv_b, conv_ln_g, conv_ln_b, w_conv_out, diff_q_norm, diff_k_norm,
           diff_lambda, diff_subln, w_diff_out, w_mem_kv, mem_q_norm, mem_k_norm,
           w_mem_out, w_o, ffn2_norm, ffn2_w_gu, ffn2_w_down):
    b, s, d = x.shape
    m = mem.shape[1]
    n = b * s
    assert d == D_MODEL and ffn1_norm.shape[0] == 1
    assert s % TM == 0 and s % TQ == 0 and s % TQ_MEM == 0 and s % KEY_CHUNK == 0
    assert (b * m) % TM == 0 and TMX % CONV_ROWS == 0 and HALO >= CONV_PAD
    assert s % TMX == 0

    w_in_b = w_in[0].astype(BF16)
    cols = [w_in_b[:, i * D_MODEL:(i + 1) * D_MODEL] for i in range(6)]
    w_gate = w_in_b[:, 6 * D_MODEL:]

    x1, h = _ffn(x.reshape(n, d), ffn1_norm[0], ffn1_w_gu[0], ffn1_w_down[0],
                 next_gain=mix_norm[0])
    h3 = h.reshape(b, s, d)

    q, k = _qk(h, cols[2], cols[3], diff_q_norm[0], diff_k_norm[0], s)
    vt, mq = _vmq(h3, cols[4].T, cols[5], mem_q_norm[0])
    score_bound = (1.01 * LOG2E * DIFF_HEAD_DIM ** 0.5
                   * jnp.max(jnp.abs(diff_q_norm[0])) * jnp.max(jnp.abs(diff_k_norm[0])))
    od = _diff_attention(q.reshape(b, s, d), k.reshape(b, s, d), vt,
                         diff_lambda[0], diff_subln[0], score_bound)

    w_kv = w_mem_kv[0].astype(BF16)
    mk, mv = _memkv(mem.reshape(b * m, d), mem_norm[0], w_kv[:, :D_MODEL],
                    w_kv[:, D_MODEL:], mem_k_norm[0])
    om = _memattn(mq, mk.reshape(b, m, d), mv.reshape(b, m, d))

    x2 = _mix(x1, h, od.reshape(n, d), om.reshape(n, d), s, cols[0], cols[1], conv_w[0],
              conv_b[0], conv_ln_g[0], conv_ln_b[0], w_conv_out[0].astype(BF16),
              w_diff_out[0].astype(BF16), w_mem_out[0].astype(BF16), w_gate, b_gate[0],
              w_o[0].astype(BF16))

    x3 = _ffn(x2, ffn2_norm[0], ffn2_w_gu[0], ffn2_w_down[0])
    return x3.reshape(b, s, d)
```

```python
import functools
import math

import jax
import jax.numpy as jnp
from jax import lax
from jax.experimental import pallas as pl
from jax.experimental.pallas import tpu as pltpu

F32 = jnp.float32
BF16 = jnp.bfloat16

D_MODEL = 1024
D_FF = 2816
CONV_WIDTH = 31
CONV_PAD = CONV_WIDTH // 2
DIFF_HEADS = 8
DIFF_HEAD_DIM = 64
MEM_HEADS = 4
MEM_HEAD_DIM = 256
ROPE_THETA = 10000.0
EPS = 1e-6
LAM_INIT = 0.8 - 0.6 * math.exp(-0.3 * 0)
LOG2E = math.log2(math.e)
Q_SCALE = DIFF_HEAD_DIM ** -0.5 * LOG2E

LANES = 128
SUBLANES = 8
BF16_SUBLANES = 16
MXU_DIM = 256
FF_CHUNK = MXU_DIM
N_FF_CHUNKS = D_FF // FF_CHUNK
VMEM_LIMIT = 56 * 1024 * 1024

TM = 1024
TMX = 512
TQ = 2048
Q_GROUP = MXU_DIM // 2
KEY_CHUNK = 2 * MXU_DIM
TQ_MEM = 2048
CONV_ROWS = 128
HALO = BF16_SUBLANES
PIPE_DEPTH = 1
MAX_SHIFT_BOUND = 60.0


def _params(n_axes):
    return pltpu.CompilerParams(
        dimension_semantics=("arbitrary",) * n_axes,
        vmem_limit_bytes=VMEM_LIMIT)


def _resident(shape):
    zeros = (0,) * len(shape)
    return pl.BlockSpec(shape, lambda *_: zeros, pipeline_mode=pl.Buffered(1))


def _rms(x, gain):
    return x * lax.rsqrt(jnp.mean(x * x, axis=-1, keepdims=True) + EPS) * gain


def _mm(a, b):
    return jnp.dot(a, b, preferred_element_type=F32)


def _mm_nt(a, b):
    return lax.dot_general(a, b, (((1,), (1,)), ((), ())),
                           preferred_element_type=F32)


def _ffn_body(emit_h, x_ref, g_ref, wg_ref, wu_ref, wd_ref, *rest):
    if emit_h:
        g2_ref, o_ref, h_ref, a_ref = rest
    else:
        o_ref, a_ref = rest
    x = x_ref[...]
    xn = _rms(x, g_ref[...]).astype(BF16)
    for c in range(N_FF_CHUNKS):
        cols = slice(c * FF_CHUNK, (c + 1) * FF_CHUNK)
        gate = _mm(xn, wg_ref[:, cols])
        up = _mm(xn, wu_ref[:, cols])
        a_ref[:, cols] = (gate * jax.nn.sigmoid(gate) * up).astype(BF16)
    x1 = x + 0.5 * _mm(a_ref[...], wd_ref[...])
    o_ref[...] = x1
    if emit_h:
        h_ref[...] = _rms(x1, g2_ref[...]).astype(BF16)


def _ffn(x, gain, w_gu, w_down, next_gain=None):
    n = x.shape[0]
    emit_h = next_gain is not None
    wg = w_gu[:, :D_FF].astype(BF16)
    wu = w_gu[:, D_FF:].astype(BF16)
    wd = w_down.astype(BF16)
    row = pl.BlockSpec((TM, D_MODEL), lambda i: (i, 0))
    in_specs = [row, _resident((1, D_MODEL)), _resident(wg.shape),
                _resident(wu.shape), _resident(wd.shape)]
    args = [x, gain.reshape(1, D_MODEL), wg, wu, wd]
    out_shape = [jax.ShapeDtypeStruct((n, D_MODEL), F32)]
    out_specs = [row]
    if emit_h:
        in_specs.append(_resident((1, D_MODEL)))
        args.append(next_gain.reshape(1, D_MODEL))
        out_shape.append(jax.ShapeDtypeStruct((n, D_MODEL), BF16))
        out_specs.append(row)
    out = pl.pallas_call(
        functools.partial(_ffn_body, emit_h),
        grid=(n // TM,),
        in_specs=in_specs, out_specs=out_specs, out_shape=out_shape,
        scratch_shapes=[pltpu.VMEM((TM, D_FF), BF16)],
        compiler_params=_params(1), name="ffn")(*args)
    return out if emit_h else out[0]


def _head_norm_rope(h, w_ref, o_ref, group_mean, direct, crossed):
    lane = lax.broadcasted_iota(jnp.int32, (h.shape[0], LANES), 1)
    upper_half = (lane & (DIFF_HEAD_DIM // 2)) != 0
    width = group_mean.shape[0]
    y = _mm(h, w_ref[...])
    for g in range(w_ref.shape[1] // width):
        yg = y[:, g * width:(g + 1) * width]
        ms = _mm((yg * yg).astype(BF16), group_mean)
        yn = yg * lax.rsqrt(ms + EPS)
        for j in range(width // LANES):
            t = yn[:, j * LANES:(j + 1) * LANES]
            swapped = jnp.where(upper_half,
                                pltpu.roll(t, DIFF_HEAD_DIM // 2, 1),
                                pltpu.roll(t, LANES - DIFF_HEAD_DIM // 2, 1))
            lanes = slice(g * width + j * LANES, g * width + (j + 1) * LANES)
            o_ref[:, lanes] = (t * direct + swapped * crossed).astype(BF16)


def _qk_body(h_ref, wq_ref, wk_ref, gm_ref, qd_ref, qc_ref, kd_ref, kc_ref,
             q_ref, k_ref):
    h = h_ref[...]
    gm = gm_ref[...]
    _head_norm_rope(h, wq_ref, q_ref, gm, qd_ref[...], qc_ref[...])
    _head_norm_rope(h, wk_ref, k_ref, gm, kd_ref[...], kc_ref[...])


def _rope_tables(gain, scale, cos, sin):
    half = DIFF_HEAD_DIM // 2
    reps = LANES // DIFF_HEAD_DIM
    swapped_gain = jnp.concatenate([gain[half:], gain[:half]])
    direct = jnp.tile(jnp.concatenate([cos, cos], axis=1) * (gain * scale), (1, reps))
    crossed = jnp.tile(jnp.concatenate([-sin, sin], axis=1) * (swapped_gain * scale),
                       (1, reps))
    return direct, crossed


def _qk(h, wq, wk, gq, gk, seq):
    n = h.shape[0]
    inv_freq = 1.0 / (ROPE_THETA ** (jnp.arange(0, DIFF_HEAD_DIM, 2, dtype=F32)
                                     / DIFF_HEAD_DIM))
    ang = jnp.arange(seq, dtype=F32)[:, None] * inv_freq[None, :]
    cos, sin = jnp.cos(ang), jnp.sin(ang)
    qd, qc = _rope_tables(gq, Q_SCALE, cos, sin)
    kd, kc = _rope_tables(gk, 1.0, cos, sin)
    ids = jnp.arange(MXU_DIM) // DIFF_HEAD_DIM
    group_mean = jnp.where(ids[:, None] == ids[None, :], 1.0 / DIFF_HEAD_DIM,
                           0.0).astype(BF16)
    row = pl.BlockSpec((TM, D_MODEL), lambda i: (i, 0))
    tab = pl.BlockSpec((TM, LANES), lambda i: (i % (seq // TM), 0))
    out = jax.ShapeDtypeStruct((n, D_MODEL), BF16)
    return pl.pallas_call(
        _qk_body, grid=(n // TM,),
        in_specs=[row, _resident(wq.shape), _resident(wk.shape),
                  _resident(group_mean.shape), tab, tab, tab, tab],
        out_specs=[row, row], out_shape=[out, out],
        compiler_params=_params(1), name="qk")(
            h, wq, wk, group_mean, qd, qc, kd, kc)


def _vmq_body(h_ref, wvt_ref, wm_ref, gm_ref, vt_ref, mq_ref):
    h = h_ref[0]
    vt_ref[0] = _mm_nt(wvt_ref[...], h).astype(BF16)
    y = _mm(h, wm_ref[...])
    scale = MEM_HEAD_DIM ** -0.5
    for g in range(MEM_HEADS):
        sl = slice(g * MEM_HEAD_DIM, (g + 1) * MEM_HEAD_DIM)
        mq_ref[0, :, sl] = (_rms(y[:, sl], gm_ref[...]) * scale).astype(BF16)


def _vmq(h, wvt, wm, gm):
    b, s, d = h.shape
    row = pl.BlockSpec((1, TM, d), lambda bi, i: (bi, i, 0))
    col = pl.BlockSpec((1, d, TM), lambda bi, i: (bi, 0, i))
    return pl.pallas_call(
        _vmq_body, grid=(b, s // TM),
        in_specs=[row, _resident(wvt.shape), _resident(wm.shape),
                  _resident((1, MEM_HEAD_DIM))],
        out_specs=[col, row],
        out_shape=[jax.ShapeDtypeStruct((b, d, s), BF16),
                   jax.ShapeDtypeStruct((b, s, d), BF16)],
        compiler_params=_params(2), name="vmq")(
            h, wvt, wm, gm.reshape(1, MEM_HEAD_DIM))


def _diff_body(bound_ref, lam_ref, sg_ref, q_ref, k_ref, vt_ref, o_ref):
    lp = lam_ref[...]
    lam = (jnp.exp(jnp.sum(lp[0:1] * lp[1:2], axis=-1, keepdims=True))
           - jnp.exp(jnp.sum(lp[2:3] * lp[3:4], axis=-1, keepdims=True)) + LAM_INIT)
    bound = bound_ref[0]
    n_keys = k_ref.shape[1]
    lane = lax.broadcasted_iota(jnp.int32, (Q_GROUP, LANES), 1)
    first = lane < DIFF_HEAD_DIM

    def shifted_by_bound(qq):
        ot = jnp.zeros((LANES, MXU_DIM), F32)
        part = jnp.zeros((SUBLANES, MXU_DIM), F32)
        for c in range(n_keys // KEY_CHUNK):
            keys = slice(c * KEY_CHUNK, (c + 1) * KEY_CHUNK)
            et = jnp.exp2(_mm_nt(k_ref[0, keys, :], qq) - bound)
            part = part + jnp.sum(
                et.reshape(KEY_CHUNK // SUBLANES, SUBLANES, MXU_DIM), axis=0)
            ot = ot + _mm(vt_ref[0, :, keys], et.astype(BF16))
        return ot / jnp.sum(part, axis=0, keepdims=True)

    def shifted_by_max(qq):
        st = _mm_nt(k_ref[0], qq)
        et = jnp.exp2(st - jnp.max(st, axis=0, keepdims=True))
        return _mm(vt_ref[0], et.astype(BF16)) / jnp.sum(et, axis=0, keepdims=True)

    def attend(softmax_pv):
        for g in range(TQ // Q_GROUP):
            rows = slice(g * Q_GROUP, (g + 1) * Q_GROUP)
            q = q_ref[0, rows, :]
            zero = jnp.zeros_like(q)
            qq = jnp.concatenate([jnp.where(first, q, zero), jnp.where(first, zero, q)],
                                 axis=0)
            ot = softmax_pv(qq)
            odt = ot[:, :Q_GROUP] - lam * ot[:, Q_GROUP:]
            ms = jnp.mean(odt * odt, axis=0, keepdims=True)
            y = odt * lax.rsqrt(ms + EPS) * sg_ref[...] * (1.0 - LAM_INIT)
            o_ref[0, rows, :] = y.T.astype(BF16)

    use_bound = bound <= MAX_SHIFT_BOUND

    @pl.when(use_bound)
    def _():
        attend(shifted_by_bound)

    @pl.when(jnp.logical_not(use_bound))
    def _():
        attend(shifted_by_max)


def _diff_attention(q, k, vt, diff_lambda, subln, score_bound):
    b, s, _ = q.shape
    width = 2 * DIFF_HEAD_DIM
    qblk = pl.BlockSpec((1, TQ, width), lambda bi, hi, i: (bi, i, hi))
    kblk = pl.BlockSpec((1, s, width), lambda bi, hi, i: (bi, 0, hi))
    vblk = pl.BlockSpec((1, width, s), lambda bi, hi, i: (bi, hi, 0))
    gain = jnp.broadcast_to(subln.reshape(width, 1), (width, Q_GROUP))
    return pl.pallas_call(
        _diff_body, grid=(b, DIFF_HEADS, s // TQ),
        in_specs=[pl.BlockSpec(memory_space=pltpu.SMEM),
                  _resident((4, DIFF_HEAD_DIM)), _resident((width, Q_GROUP)),
                  qblk, kblk, vblk],
        out_specs=qblk, out_shape=jax.ShapeDtypeStruct(q.shape, BF16),
        compiler_params=_params(3), name="diffattn")(
            score_bound.reshape(1), diff_lambda, gain, q, k, vt)


def _memkv_body(m_ref, g_ref, wk_ref, wv_ref, gk_ref, k_ref, v_ref):
    xn = _rms(m_ref[...], g_ref[...]).astype(BF16)
    v_ref[...] = _mm(xn, wv_ref[...]).astype(BF16)
    y = _mm(xn, wk_ref[...])
    for g in range(MEM_HEADS):
        sl = slice(g * MEM_HEAD_DIM, (g + 1) * MEM_HEAD_DIM)
        k_ref[:, sl] = _rms(y[:, sl], gk_ref[...]).astype(BF16)


def _memkv(mem, gain, wk, wv, gk):
    n = mem.shape[0]
    row = pl.BlockSpec((TM, D_MODEL), lambda i: (i, 0))
    out = jax.ShapeDtypeStruct((n, D_MODEL), BF16)
    return pl.pallas_call(
        _memkv_body, grid=(n // TM,),
        in_specs=[row, _resident((1, D_MODEL)), _resident(wk.shape),
                  _resident(wv.shape), _resident((1, MEM_HEAD_DIM))],
        out_specs=[row, row], out_shape=[out, out],
        compiler_params=_params(1), name="memkv")(
            mem, gain.reshape(1, D_MODEL), wk, wv, gk.reshape(1, MEM_HEAD_DIM))


def _memattn_body(q_ref, k_ref, v_ref, o_ref):
    for g in range(MEM_HEADS):
        sl = slice(g * MEM_HEAD_DIM, (g + 1) * MEM_HEAD_DIM)
        s = _mm_nt(q_ref[0, :, sl], k_ref[0, :, sl])
        e = jnp.exp(s - jnp.max(s, axis=-1, keepdims=True))
        l = jnp.sum(e, axis=-1, keepdims=True)
        o_ref[0, :, sl] = (_mm(e.astype(BF16), v_ref[0, :, sl]) / l).astype(BF16)


def _memattn(q, k, v):
    b, s, w = q.shape
    m = k.shape[1]
    qblk = pl.BlockSpec((1, TQ_MEM, w), lambda bi, i: (bi, i, 0))
    kvblk = pl.BlockSpec((1, m, w), lambda bi, i: (bi, 0, 0))
    return pl.pallas_call(
        _memattn_body, grid=(b, s // TQ_MEM),
        in_specs=[qblk, kvblk, kvblk], out_specs=qblk,
        out_shape=jax.ShapeDtypeStruct(q.shape, BF16),
        compiler_params=_params(2), name="memattn")(q, k, v)


def _mix_body(tiles_per_seq, hp_ref, hc_ref, hn_ref, x_ref, h_ref, d_ref, m_ref, wa_ref,
              wb_ref, cw_ref, cb_ref, lg_ref, lb_ref, wc_ref, wd_ref, wm_ref, wg_ref,
              bg_ref, wo_ref, o_ref, xp_ref, xs_ref, y_ref):
    t = pl.program_id(0)
    rows = TMX + 2 * HALO

    @pl.when(t == 0)
    def _():
        y_ref[...] = jnp.zeros_like(y_ref)

    n_lane_tiles = D_MODEL // LANES
    y = jnp.concatenate([y_ref[j] for j in range(n_lane_tiles)], axis=1) + cb_ref[...]

    pos = jnp.minimum(t, pl.num_programs(0) - 1 - PIPE_DEPTH) % tiles_per_seq
    hcat = jnp.concatenate([hp_ref[...], hc_ref[...], hn_ref[...]], axis=0)
    u = _mm(hcat, wa_ref[...]) * jax.nn.sigmoid(_mm(hcat, wb_ref[...]))
    r = lax.broadcasted_iota(jnp.int32, (rows, 1), 0)
    lo = jnp.where(pos > 0, 0, HALO)
    hi = jnp.where(pos < tiles_per_seq - 1, rows, HALO + TMX)
    u = jnp.where((r >= lo) & (r < hi), u, 0.0)
    for j in range(n_lane_tiles):
        xp_ref[j] = u[:, j * LANES:(j + 1) * LANES]

    base = HALO - CONV_PAD
    span = rows - SUBLANES
    for j in range(n_lane_tiles):
        lanes = slice(j * LANES, (j + 1) * LANES)
        for p in range(1, SUBLANES):
            xs_ref[p - 1, 0:span, :] = xp_ref[j, p:p + span, :]
        for r0 in range(0, TMX, CONV_ROWS):
            acc = jnp.zeros((CONV_ROWS, LANES), F32)
            for k in range(CONV_WIDTH):
                p = (base + k) % SUBLANES
                start = base + k - p + r0
                if p == 0:
                    win = xp_ref[j, start:start + CONV_ROWS, :]
                else:
                    win = xs_ref[p - 1, start:start + CONV_ROWS, :]
                acc = acc + win * cw_ref[k:k + 1, lanes]
            y_ref[j, r0:r0 + CONV_ROWS, :] = acc

    mu = jnp.mean(y, axis=-1, keepdims=True)
    yc = y - mu
    var = jnp.mean(yc * yc, axis=-1, keepdims=True)
    z = yc * lax.rsqrt(var + EPS) * lg_ref[...] + lb_ref[...]
    c = (z * jax.nn.sigmoid(z)).astype(BF16)
    gates = jax.nn.sigmoid(_mm(h_ref[...], wg_ref[...]) + bg_ref[...])
    merged = (gates[:, :D_MODEL] * _mm(c, wc_ref[...])
              + gates[:, D_MODEL:2 * D_MODEL] * _mm(d_ref[...], wd_ref[...])
              + gates[:, 2 * D_MODEL:] * _mm(m_ref[...], wm_ref[...]))
    o_ref[...] = x_ref[...] + _mm(merged.astype(BF16), wo_ref[...])


def _mix(x1, h, od, om, seq, wa, wb, conv_w, conv_b, ln_g, ln_b, wc, wd, wm, wg, bg, wo):
    n, d = x1.shape
    tiles = n // TMX
    per = TMX // HALO
    last_halo = n // HALO - 1

    def conv_tile(t):
        return jnp.minimum(t, tiles - 1)

    conv_row = pl.BlockSpec((TMX, d), lambda t: (conv_tile(t), 0))
    prev = pl.BlockSpec((HALO, d), lambda t: (jnp.maximum(conv_tile(t) * per - 1, 0), 0))
    nxt = pl.BlockSpec((HALO, d),
                       lambda t: (jnp.minimum((conv_tile(t) + 1) * per, last_halo), 0))
    row = pl.BlockSpec((TMX, d), lambda t: (jnp.maximum(t - PIPE_DEPTH, 0), 0))
    vec = _resident((1, d))
    return pl.pallas_call(
        functools.partial(_mix_body, seq // TMX), grid=(tiles + PIPE_DEPTH,),
        in_specs=[prev, conv_row, nxt, row, row, row, row, _resident(wa.shape),
                  _resident(wb.shape), _resident((CONV_WIDTH, d)), vec, vec, vec,
                  _resident(wc.shape), _resident(wd.shape), _resident(wm.shape),
                  _resident(wg.shape), _resident((1, 3 * d)), _resident(wo.shape)],
        out_specs=row, out_shape=jax.ShapeDtypeStruct((n, d), F32),
        scratch_shapes=[pltpu.VMEM((d // LANES, TMX + 2 * HALO, LANES), F32),
                        pltpu.VMEM((SUBLANES - 1, TMX + 2 * HALO, LANES), F32),
                        pltpu.VMEM((d // LANES, TMX, LANES), F32)],
        compiler_params=_params(1), name="mix")(
            h, h, h, x1, h, od, om, wa, wb, conv_w.reshape(CONV_WIDTH, d),
            conv_b.reshape(1, d), ln_g.reshape(1, d), ln_b.reshape(1, d),
            wc, wd, wm, wg, bg.reshape(1, 3 * d), wo)


def kernel(x, mem, ffn1_norm, ffn1_w_gu, ffn1_w_down, mix_norm, mem_norm, w_in, b_gate,
           conv_w, conv_b, conv_ln_g, conv_ln_b, w_conv_out, diff_q_norm, diff_k_norm,
           diff_lambda, diff_subln, w_diff_out, w_mem_kv, mem_q_norm, mem_k_norm,
           w_mem_out, w_o, ffn2_norm, ffn2_w_gu, ffn2_w_down):
    b, s, d = x.shape
    m = mem.shape[1]
    n = b * s
    assert d == D_MODEL and ffn1_norm.shape[0] == 1
    assert s % TM == 0 and s % TQ == 0 and s % TQ_MEM == 0 and s % KEY_CHUNK == 0
    assert (b * m) % TM == 0 and TMX % CONV_ROWS == 0 and HALO >= CONV_PAD
    assert s % TMX == 0

    w_in_b = w_in[0].astype(BF16)
    cols = [w_in_b[:, i * D_MODEL:(i + 1) * D_MODEL] for i in range(6)]
    w_gate = w_in_b[:, 6 * D_MODEL:]

    x1, h = _ffn(x.reshape(n, d), ffn1_norm[0], ffn1_w_gu[0], ffn1_w_down[0],
                 next_gain=mix_norm[0])
    h3 = h.reshape(b, s, d)

    q, k = _qk(h, cols[2], cols[3], diff_q_norm[0], diff_k_norm[0], s)
    vt, mq = _vmq(h3, cols[4].T, cols[5], mem_q_norm[0])
    score_bound = (1.01 * LOG2E * DIFF_HEAD_DIM ** 0.5
                   * jnp.max(jnp.abs(diff_q_norm[0])) * jnp.max(jnp.abs(diff_k_norm[0])))
    od = _diff_attention(q.reshape(b, s, d), k.reshape(b, s, d), vt,
                         diff_lambda[0], diff_subln[0], score_bound)

    w_kv = w_mem_kv[0].astype(BF16)
    mk, mv = _memkv(mem.reshape(b * m, d), mem_norm[0], w_kv[:, :D_MODEL],
                    w_kv[:, D_MODEL:], mem_k_norm[0])
    om = _memattn(mq, mk.reshape(b, m, d), mv.reshape(b, m, d))

    x2 = _mix(x1, h, od.reshape(n, d), om.reshape(n, d), s, cols[0], cols[1], conv_w[0],
              conv_b[0], conv_ln_g[0], conv_ln_b[0], w_conv_out[0].astype(BF16),
              w_diff_out[0].astype(BF16), w_mem_out[0].astype(BF16), w_gate, b_gate[0],
              w_o[0].astype(BF16))

    x3 = _ffn(x2, ffn2_norm[0], ffn2_w_gu[0], ffn2_w_down[0])
    return x3.reshape(b, s, d)
```
